```python
import functools
import jax, jax.numpy as jnp
from jax import lax
import numpy as np

D_MODEL = 2048
BATCH = 4
SEQ = 2048
DEPTH = 2
DEC_BATCH = 128
DEC_SEQ = 1
PAST_LEN = 16384
PAGE_SIZE = 128

N_EVEN = (DEPTH + 1) // 2
N_ODD = DEPTH // 2
D_FF = 5632
P_DIM = 256
CONV_A_WIDTH = 1024
CONV_B_WIDTH = 1024
KA = 31
KB = 3
IN_COLS = 2 * CONV_A_WIDTH + 3 * CONV_B_WIDTH
MIX_OUT = CONV_A_WIDTH + CONV_B_WIDTH
N_HEADS = 16
QK_NOPE = 128
QK_ROPE = 64
QK_DIM = QK_NOPE + QK_ROPE
V_DIM = 128
Q_LORA = 512
KV_LORA = 512
ROPE_THETA = 10000.0
Q_BLOCK = 128
EPS = 1e-6
ATTN_SCALE = QK_DIM ** -0.5
NEG = -1e30

kernel_name = 'hybrid_conformer_shortconv_mla_decoder_step'


def rmsnorm(x, g):
    xf = x.astype(jnp.float32)
    y = xf * lax.rsqrt(jnp.mean(xf * xf, axis=-1, keepdims=True) + EPS)
    return (y * g.astype(jnp.float32)).astype(x.dtype)


def layernorm(x, g, b):
    xf = x.astype(jnp.float32)
    xc = xf - jnp.mean(xf, axis=-1, keepdims=True)
    y = xc * lax.rsqrt(jnp.mean(xc * xc, axis=-1, keepdims=True) + EPS)
    return (y * g.astype(jnp.float32) + b.astype(jnp.float32)).astype(x.dtype)


def swiglu(h, w_gu, w_down):
    g, u = jnp.split(h @ w_gu, 2, axis=-1)
    return (jax.nn.silu(g) * u) @ w_down


def rope(x, pos):
    half = x.shape[-1] // 2
    inv_freq = ROPE_THETA ** (-jnp.arange(half, dtype=jnp.float32) / half)
    ang = pos.astype(jnp.float32)[:, None] * inv_freq[None, :]
    shape = (1, pos.shape[0]) + (1,) * (x.ndim - 3) + (half,)
    cos = jnp.cos(ang).reshape(shape)
    sin = jnp.sin(ang).reshape(shape)
    xf = x.astype(jnp.float32)
    x1, x2 = xf[..., :half], xf[..., half:]
    return jnp.concatenate([x1 * cos - x2 * sin, x2 * cos + x1 * sin], axis=-1).astype(x.dtype)


def causal_dwconv(xpad, w):
    return lax.conv_general_dilated(xpad, w[:, None, :].astype(xpad.dtype), window_strides=(1,), padding='VALID',
                                    dimension_numbers=('NWC', 'WIO', 'NWC'), feature_group_count=xpad.shape[-1])


def conv_mixer(h, buf_a, buf_b, cw):
    w_in, dw_a, dw_a_bias, ln_g, ln_b, dw_b, w_out = cw
    z = h @ w_in
    a_val, a_gate, b_h, b_gb, b_gc = jnp.split(
        z, [CONV_A_WIDTH, 2 * CONV_A_WIDTH, 2 * CONV_A_WIDTH + CONV_B_WIDTH,
            2 * CONV_A_WIDTH + 2 * CONV_B_WIDTH], axis=-1)
    ua = a_val * jax.nn.sigmoid(a_gate)
    ua_pad = jnp.concatenate([buf_a, ua], axis=1)
    ya = causal_dwconv(ua_pad, dw_a) + dw_a_bias
    ya = jax.nn.silu(layernorm(ya, ln_g, ln_b))
    ub = b_gc * b_h
    ub_pad = jnp.concatenate([buf_b, ub], axis=1)
    yb = b_gb * causal_dwconv(ub_pad, dw_b)
    y = jnp.concatenate([ya, yb], axis=-1) @ w_out
    return y, (ua_pad[:, -(KA - 1):], ub_pad[:, -(KB - 1):])


def mla_project(h, pos, mw):
    w_dq, q_norm, w_uq, w_dkv, kv_norm, w_uk, gain_q, gain_k, w_uv, w_o = mw
    cq = rmsnorm(h @ w_dq, q_norm)
    q = rmsnorm(jnp.einsum('btc,chd->bthd', cq, w_uq), gain_q)
    q_nope = q[..., :QK_NOPE]
    q_rope = rope(q[..., QK_NOPE:], pos)
    kv = h @ w_dkv
    ckv = rmsnorm(kv[..., :KV_LORA], kv_norm)
    kr_raw = kv[..., KV_LORA:]
    k_nope = jnp.einsum('btc,chd->bthd', ckv, w_uk)
    ssq = jnp.sum(jnp.square(k_nope.astype(jnp.float32)), axis=-1) + \
        jnp.sum(jnp.square(kr_raw.astype(jnp.float32)), axis=-1)[..., None]
    ks = lax.rsqrt(ssq / QK_DIM + EPS).astype(h.dtype)
    kr = rope(kr_raw * gain_k[QK_NOPE:], pos)
    q_lat = jnp.einsum('bthd,chd->bthc', q_nope * gain_k[:QK_NOPE], w_uk)
    return q_lat, q_rope, ckv, kr, ks


def latent_scores(q_lat, q_rope, ckv, kr, ks):
    s = jnp.einsum('bthc,bkc->bhtk', q_lat, ckv, preferred_element_type=jnp.float32) + \
        jnp.einsum('bthr,bkr->bhtk', q_rope, kr, preferred_element_type=jnp.float32)
    return s * (jnp.swapaxes(ks.astype(jnp.float32), 1, 2)[:, :, None, :] * ATTN_SCALE)


def online_update(carry, s, ck):
    m, l, acc = carry
    m_new = jnp.maximum(m, jnp.max(s, axis=-1))
    alpha = jnp.exp(m - m_new)
    p = jnp.exp(s - m_new[..., None])
    l = l * alpha + jnp.sum(p, axis=-1)
    acc = acc * alpha[..., None] + jnp.einsum('bhtk,bkc->bhtc', p, ck.astype(jnp.float32))
    return (m_new, l, acc)


def mla_output(o_lat, mw):
    w_uv, w_o = mw[8], mw[9]
    o = jnp.einsum('bthc,chd->bthd', o_lat, w_uv)
    return o.reshape(o.shape[0], o.shape[1], N_HEADS * V_DIM) @ w_o


def mla_prompt_mixer(h, pos, mw):
    q_lat, q_rope, ckv, kr, ks = mla_project(h, pos, mw)
    bsz, seq = h.shape[0], h.shape[1]
    nb = seq // Q_BLOCK
    key_pos = jnp.arange(seq)

    def one_block(args):
        ql, qr, qpos = args
        s = latent_scores(ql, qr, ckv, kr, ks)
        s = jnp.where(key_pos[None, :] <= qpos[:, None], s, NEG)
        p = jax.nn.softmax(s, axis=-1)
        return jnp.einsum('bhtk,bkc->bthc', p, ckv.astype(jnp.float32)).astype(ql.dtype)

    def to_blocks(a):
        return jnp.swapaxes(a.reshape((bsz, nb, Q_BLOCK) + a.shape[2:]), 0, 1)

    o = lax.map(one_block, (to_blocks(q_lat), to_blocks(q_rope), key_pos.reshape(nb, Q_BLOCK)))
    o_lat = jnp.swapaxes(o, 0, 1).reshape((bsz, seq) + o.shape[3:])

    def to_pages(a):
        return a.reshape((bsz, seq // PAGE_SIZE, PAGE_SIZE) + a.shape[2:])

    return mla_output(o_lat, mw), (to_pages(ckv), to_pages(kr), to_pages(ks))


def mla_sample_mixer(h, pos, mw, cache_ckv, cache_krope, cache_kscale, page_table, li):
    q_lat, q_rope, ckv, kr, ks = mla_project(h, pos, mw)
    bd, t = h.shape[0], h.shape[1]
    init = (jnp.full((bd, N_HEADS, t), NEG, jnp.float32),
            jnp.zeros((bd, N_HEADS, t), jnp.float32),
            jnp.zeros((bd, N_HEADS, t, KV_LORA), jnp.float32))

    def page_step(carry, pages):
        ck = cache_ckv[li, pages]
        s = latent_scores(q_lat, q_rope, ck, cache_krope[li, pages], cache_kscale[li, pages])
        return online_update(carry, s, ck), None

    carry, _ = lax.scan(page_step, init, jnp.swapaxes(page_table, 0, 1))
    s_self = latent_scores(q_lat, q_rope, ckv, kr, ks)
    causal = jnp.arange(t)[None, :] <= jnp.arange(t)[:, None]
    m, l, acc = online_update(carry, jnp.where(causal, s_self, NEG), ckv)
    o_lat = jnp.swapaxes(acc / l[..., None], 1, 2).astype(h.dtype)
    return mla_output(o_lat, mw), (ckv, kr, ks)


def macaron_layer(x, p_i, lw, mixer):
    norms, w_gu, w_down, ple_gate, ple_proj = lw
    x = x + 0.5 * swiglu(rmsnorm(x, norms[0]), w_gu[0], w_down[0])
    y, new_state = mixer(rmsnorm(x, norms[1]))
    x = x + y
    x = x + 0.5 * swiglu(rmsnorm(x, norms[2]), w_gu[1], w_down[1])
    gate = jax.nn.sigmoid(rmsnorm(x, norms[3]) @ ple_gate)
    x = x + gate * (p_i @ ple_proj)
    return x, new_state


def setup_inputs(seed: int = 0) -> dict:
    key = jax.random.key(seed)
    keys = iter(jax.random.split(key, 40))

    def nrm(shape, scale):
        return jax.random.normal(next(keys), shape, jnp.float32) * scale

    def gain(shape):
        return 1.0 + nrm(shape, 0.02)

    D = D_MODEL
    n_pages = PAST_LEN // PAGE_SIZE
    n_used = DEC_BATCH * n_pages
    n_phys = n_used + n_used // 4
    perm = jax.random.permutation(next(keys), n_phys)
    page_table = perm[:n_used].reshape(DEC_BATCH, n_pages).astype(jnp.int32)
    return {
        'x_prompt': nrm((BATCH, SEQ, D), 1.0),
        'x_sample': nrm((DEC_BATCH, DEC_SEQ, D), 1.0),
        'p_prompt': nrm((DEPTH, BATCH, SEQ, P_DIM), 1.0),
        'p_sample': nrm((DEPTH, DEC_BATCH, DEC_SEQ, P_DIM), 1.0),
        'state_conv_a': nrm((N_EVEN, DEC_BATCH, KA - 1, CONV_A_WIDTH), 0.5),
        'state_conv_b': nrm((N_EVEN, DEC_BATCH, KB - 1, CONV_B_WIDTH), 0.5),
        'cache_ckv': nrm((N_ODD, n_phys, PAGE_SIZE, KV_LORA), 1.0),
        'cache_krope': nrm((N_ODD, n_phys, PAGE_SIZE, QK_ROPE), 1.0),
        'cache_kscale': jax.random.uniform(next(keys), (N_ODD, n_phys, PAGE_SIZE, N_HEADS), jnp.float32, 0.8, 1.2),
        'page_table': page_table,
        'norm_gains': gain((DEPTH, 4, D)),
        'ffn_w_gu': nrm((DEPTH, 2, D, 2 * D_FF), D ** -0.5),
        'ffn_w_down': nrm((DEPTH, 2, D_FF, D), D_FF ** -0.5),
        'ple_w_gate': nrm((DEPTH, D, D), D ** -0.5),
        'ple_w_proj': nrm((DEPTH, P_DIM, D), P_DIM ** -0.5),
        'conv_w_in': nrm((N_EVEN, D, IN_COLS), D ** -0.5),
        'conv_dw_a': nrm((N_EVEN, KA, CONV_A_WIDTH), KA ** -0.5),
        'conv_dw_a_bias': nrm((N_EVEN, CONV_A_WIDTH), 0.01),
        'conv_ln_gain': gain((N_EVEN, CONV_A_WIDTH)),
        'conv_ln_bias': nrm((N_EVEN, CONV_A_WIDTH), 0.01),
        'conv_dw_b': nrm((N_EVEN, KB, CONV_B_WIDTH), KB ** -0.5),
        'conv_w_out': nrm((N_EVEN, MIX_OUT, D), MIX_OUT ** -0.5),
        'mla_w_dq': nrm((N_ODD, D, Q_LORA), D ** -0.5),
        'mla_q_norm': gain((N_ODD, Q_LORA)),
        'mla_w_uq': nrm((N_ODD, Q_LORA, N_HEADS, QK_DIM), Q_LORA ** -0.5),
        'mla_w_dkv': nrm((N_ODD, D, KV_LORA + QK_ROPE), D ** -0.5),
        'mla_kv_norm': gain((N_ODD, KV_LORA)),
        'mla_w_uk': nrm((N_ODD, KV_LORA, N_HEADS, QK_NOPE), KV_LORA ** -0.5),
        'mla_w_uv': nrm((N_ODD, KV_LORA, N_HEADS, V_DIM), KV_LORA ** -0.5),
        'mla_qk_gain_q': gain((N_ODD, QK_DIM)),
        'mla_qk_gain_k': gain((N_ODD, QK_DIM)),
        'mla_w_o': nrm((N_ODD, N_HEADS * V_DIM, D), (N_HEADS * V_DIM) ** -0.5),
    }


def reference(x_prompt, x_sample, p_prompt, p_sample, state_conv_a, state_conv_b, cache_ckv, cache_krope,
              cache_kscale, page_table, norm_gains, ffn_w_gu, ffn_w_down, ple_w_gate, ple_w_proj, conv_w_in,
              conv_dw_a, conv_dw_a_bias, conv_ln_gain, conv_ln_bias, conv_dw_b, conv_w_out, mla_w_dq, mla_q_norm,
              mla_w_uq, mla_w_dkv, mla_kv_norm, mla_w_uk, mla_w_uv, mla_qk_gain_q, mla_qk_gain_k, mla_w_o):
    pos_p = jnp.arange(x_prompt.shape[1], dtype=jnp.int32)
    pos_s = PAST_LEN + jnp.arange(x_sample.shape[1], dtype=jnp.int32)
    xp, xs = x_prompt, x_sample
    ca_p, ca_s, cb_p, cb_s = [], [], [], []
    ckv_p, ckv_s, kr_p, kr_s, ks_p, ks_s = [], [], [], [], [], []
    for i in range(DEPTH):
        j = i // 2
        lw = (norm_gains[i], ffn_w_gu[i], ffn_w_down[i], ple_w_gate[i], ple_w_proj[i])
        if i % 2 == 0:
            cw = (conv_w_in[j], conv_dw_a[j], conv_dw_a_bias[j], conv_ln_gain[j], conv_ln_bias[j],
                  conv_dw_b[j], conv_w_out[j])
            zero_a = jnp.zeros((xp.shape[0], KA - 1, CONV_A_WIDTH), xp.dtype)
            zero_b = jnp.zeros((xp.shape[0], KB - 1, CONV_B_WIDTH), xp.dtype)
            xp, (sa, sb) = macaron_layer(xp, p_prompt[i], lw,
                                         functools.partial(conv_mixer, buf_a=zero_a, buf_b=zero_b, cw=cw))
            xs, (ta, tb) = macaron_layer(xs, p_sample[i], lw,
                                         functools.partial(conv_mixer, buf_a=state_conv_a[j],
                                                           buf_b=state_conv_b[j], cw=cw))
            ca_p.append(sa); cb_p.append(sb); ca_s.append(ta); cb_s.append(tb)
        else:
            mw = (mla_w_dq[j], mla_q_norm[j], mla_w_uq[j], mla_w_dkv[j], mla_kv_norm[j], mla_w_uk[j],
                  mla_qk_gain_q[j], mla_qk_gain_k[j], mla_w_uv[j], mla_w_o[j])
            xp, (c1, r1, k1) = macaron_layer(xp, p_prompt[i], lw,
                                             functools.partial(mla_prompt_mixer, pos=pos_p, mw=mw))
            xs, (c2, r2, k2) = macaron_layer(xs, p_sample[i], lw,
                                             functools.partial(mla_sample_mixer, pos=pos_s, mw=mw,
                                                               cache_ckv=cache_ckv, cache_krope=cache_krope,
                                                               cache_kscale=cache_kscale,
                                                               page_table=page_table, li=j))
            ckv_p.append(c1); kr_p.append(r1); ks_p.append(k1)
            ckv_s.append(c2); kr_s.append(r2); ks_s.append(k2)
    conv_a_prompt, conv_a_sample = jnp.stack(ca_p), jnp.stack(ca_s)
    conv_b_prompt, conv_b_sample = jnp.stack(cb_p), jnp.stack(cb_s)
    ckv_prompt, ckv_sample = jnp.stack(ckv_p), jnp.stack(ckv_s)
    krope_prompt, krope_sample = jnp.stack(kr_p), jnp.stack(kr_s)
    kscale_prompt, kscale_sample = jnp.stack(ks_p), jnp.stack(ks_s)
    return (xp, xs, conv_a_prompt, conv_a_sample, conv_b_prompt, conv_b_sample, ckv_prompt, ckv_sample,
            krope_prompt, krope_sample, kscale_prompt, kscale_sample)
```

```python
import functools

import jax
import jax.numpy as jnp
from jax import lax
from jax.experimental import pallas as pl
from jax.experimental.pallas import tpu as pltpu

F32 = jnp.float32
BF16 = jnp.bfloat16

EPS = 1e-6
ROPE_THETA = 10000.0
NEG = -1e30
PAGE_SIZE = 128
N_HEADS = 16
QK_NOPE = 128
QK_ROPE = 64
QK_DIM = QK_NOPE + QK_ROPE
KV_LORA = 512
KA = 31
KB = 3
CONV_WIDTH = 1024
ATTN_SCALE = QK_DIM ** -0.5
LANES = 128
VMEM_LIMIT = 56 * 1024 * 1024
PAGES_PER_CHUNK = 8


def _params(semantics):
    return pltpu.CompilerParams(dimension_semantics=semantics, vmem_limit_bytes=VMEM_LIMIT)


def _resident(shape):
    nd = len(shape)
    return pl.BlockSpec(shape, lambda *_: (0,) * nd, pipeline_mode=pl.Buffered(1))


def _rms(x, g):
    return x * lax.rsqrt(jnp.mean(x * x, axis=-1, keepdims=True) + EPS) * g


def _dot(a, b):
    return jnp.dot(a, b, preferred_element_type=F32)


def _dot_nt(a, b):
    return lax.dot_general(a, b, (((1,), (1,)), ((), ())), preferred_element_type=F32)


def _hl_dot(v, sel):
    hi = v.astype(BF16)
    lo = (v - hi.astype(F32)).astype(BF16)
    return _dot(hi, sel) + _dot(lo, sel)


def _ffn_body(x_ref, g_ref, wg_ref, wu_ref, wd_ref, o_ref, h_ref):
    @pl.when(pl.program_id(1) == 0)
    def _():
        x = x_ref[...]
        h_ref[...] = _rms(x, g_ref[...]).astype(BF16)
        o_ref[...] = x

    h = h_ref[...]
    g = _dot(h, wg_ref[...])
    u = _dot(h, wu_ref[...])
    a = (0.5 * g * jax.nn.sigmoid(g) * u).astype(BF16)
    o_ref[...] += _dot(a, wd_ref[...])


def _ffn(x, gain, w_gu, w_down, bm, bf):
    n, d = x.shape
    nf = w_down.shape[0] // bf
    return pl.pallas_call(
        _ffn_body,
        grid=(n // bm, nf),
        in_specs=[pl.BlockSpec((bm, d), lambda m, f: (m, 0)),
                  pl.BlockSpec((1, d), lambda m, f: (0, 0)),
                  pl.BlockSpec((d, bf), lambda m, f: (0, f)),
                  pl.BlockSpec((d, bf), lambda m, f: (0, nf + f)),
                  pl.BlockSpec((bf, d), lambda m, f: (f, 0))],
        out_specs=pl.BlockSpec((bm, d), lambda m, f: (m, 0)),
        out_shape=jax.ShapeDtypeStruct((n, d), F32),
        scratch_shapes=[pltpu.VMEM((bm, d), BF16)],
        compiler_params=_params(("parallel", "arbitrary")),
        name="ffn",
    )(x, gain, w_gu, w_gu, w_down)


def _ple_body(x_ref, g_ref, wg_ref, p_ref, wp_ref, o_ref):
    x = x_ref[...]
    h = _rms(x, g_ref[...]).astype(BF16)
    gate = jax.nn.sigmoid(_dot(h, wg_ref[...]))
    o_ref[...] = x + gate * _dot(p_ref[...].astype(BF16), wp_ref[...])


def _ple(x, gain, w_gate, p, w_proj, bm):
    n, d = x.shape
    pd = p.shape[1]
    return pl.pallas_call(
        _ple_body,
        grid=(n // bm,),
        in_specs=[pl.BlockSpec((bm, d), lambda m: (m, 0)),
                  _resident((1, d)),
                  _resident((d, d)),
                  pl.BlockSpec((bm, pd), lambda m: (m, 0)),
                  _resident((pd, d))],
        out_specs=pl.BlockSpec((bm, d), lambda m: (m, 0)),
        out_shape=jax.ShapeDtypeStruct((n, d), F32),
        compiler_params=_params(("parallel",)),
        name="ple",
    )(x, gain, w_gate, p, w_proj)


def _conv_in_body(x_ref, g_ref, wav_ref, wag_ref, wbh_ref, wbb_ref, wbc_ref, ua_ref, ub_ref, gb_ref, h_ref):
    @pl.when(pl.program_id(1) == 0)
    def _():
        h_ref[...] = _rms(x_ref[...], g_ref[...]).astype(BF16)

    h = h_ref[...]
    ua_ref[...] = _dot(h, wav_ref[...]) * jax.nn.sigmoid(_dot(h, wag_ref[...]))
    ub_ref[...] = _dot(h, wbc_ref[...]) * _dot(h, wbh_ref[...])
    gb_ref[...] = _dot(h, wbb_ref[...])


def _conv_in(x, gain, w_in, bm, bc):
    n, d = x.shape
    c = CONV_WIDTH
    nc = c // bc

    def wspec(k):
        return pl.BlockSpec((d, bc), lambda m, j: (0, k * nc + j))

    out = jax.ShapeDtypeStruct((n, c), F32)
    ospec = pl.BlockSpec((bm, bc), lambda m, j: (m, j))
    return pl.pallas_call(
        _conv_in_body,
        grid=(n // bm, nc),
        in_specs=[pl.BlockSpec((bm, d), lambda m, j: (m, 0)),
                  pl.BlockSpec((1, d), lambda m, j: (0, 0)),
                  wspec(0), wspec(1), wspec(2), wspec(3), wspec(4)],
        out_specs=[ospec, ospec, ospec],
        out_shape=[out, out, out],
        scratch_shapes=[pltpu.VMEM((bm, d), BF16)],
        compiler_params=_params(("parallel", "arbitrary")),
        name="conv_in",
    )(x, gain, w_in, w_in, w_in, w_in, w_in)


def _ln_silu(y, g, b):
    yc = y - jnp.mean(y, axis=-1, keepdims=True)
    z = yc * lax.rsqrt(jnp.mean(yc * yc, axis=-1, keepdims=True) + EPS) * g + b
    return z * jax.nn.sigmoid(z)


HALO_A = 32
HALO_B = 8
CONV_ROWS = 64


def _conv_p_body(ua_ref, uah_ref, ub_ref, ubh_ref, gb_ref, x_ref, dwa_ref, ba_ref, lng_ref, lnb_ref, dwb_ref,
                 wo_ref, o_ref, pa_ref, pb_ref, ya_ref, yc_ref, *, bt):
    first = pl.program_id(1) == 0
    pa_ref[0:HALO_A, :] = jnp.where(first, 0.0, uah_ref[...])
    pa_ref[HALO_A:HALO_A + bt, :] = ua_ref[...]
    pb_ref[0:HALO_B, :] = jnp.where(first, 0.0, ubh_ref[...])
    pb_ref[HALO_B:HALO_B + bt, :] = ub_ref[...]
    c = pa_ref.shape[1]

    def chunk(cc, carry):
        cs = pl.multiple_of(cc * LANES, LANES)
        wa = dwa_ref[:, pl.ds(cs, LANES)]
        wb = dwb_ref[:, pl.ds(cs, LANES)]
        for r in range(bt // CONV_ROWS):
            acc = jnp.zeros((CONV_ROWS, LANES), F32)
            for k in range(KA):
                acc = acc + wa[k:k + 1, :] * pa_ref[pl.ds(r * CONV_ROWS + HALO_A - (KA - 1) + k, CONV_ROWS),
                                                    pl.ds(cs, LANES)]
            ya_ref[pl.ds(r * CONV_ROWS, CONV_ROWS), pl.ds(cs, LANES)] = acc
            accb = jnp.zeros((CONV_ROWS, LANES), F32)
            for k in range(KB):
                accb = accb + wb[k:k + 1, :] * pb_ref[pl.ds(r * CONV_ROWS + HALO_B - (KB - 1) + k, CONV_ROWS),
                                                      pl.ds(cs, LANES)]
            yb = gb_ref[pl.ds(r * CONV_ROWS, CONV_ROWS), pl.ds(cs, LANES)] * accb
            yc_ref[pl.ds(r * CONV_ROWS, CONV_ROWS), pl.ds(c + cs, LANES)] = yb.astype(BF16)
        return carry

    lax.fori_loop(0, c // LANES, chunk, 0)
    ya = _ln_silu(ya_ref[...] + ba_ref[...], lng_ref[...], lnb_ref[...])
    yc_ref[:, 0:c] = ya.astype(BF16)
    o_ref[...] = x_ref[...] + _dot(yc_ref[...], wo_ref[...])


def _conv_prompt(ua, ub, gb, x, dwa, ba, lng, lnb, dwb, w_out, bt):
    b, t, c = ua.shape
    d = x.shape[2]
    body = functools.partial(_conv_p_body, bt=bt)
    cur = pl.BlockSpec((None, bt, c), lambda i, j: (i, j, 0))
    return pl.pallas_call(
        body,
        grid=(b, t // bt),
        in_specs=[cur,
                  pl.BlockSpec((None, HALO_A, c), lambda i, j: (i, jnp.maximum(j * (bt // HALO_A) - 1, 0), 0)),
                  cur,
                  pl.BlockSpec((None, HALO_B, c), lambda i, j: (i, jnp.maximum(j * (bt // HALO_B) - 1, 0), 0)),
                  cur,
                  pl.BlockSpec((None, bt, d), lambda i, j: (i, j, 0)),
                  _resident(dwa.shape), _resident(ba.shape), _resident(lng.shape), _resident(lnb.shape),
                  _resident(dwb.shape), _resident(w_out.shape)],
        out_specs=pl.BlockSpec((None, bt, d), lambda i, j: (i, j, 0)),
        out_shape=jax.ShapeDtypeStruct((b, t, d), F32),
        scratch_shapes=[pltpu.VMEM((HALO_A + bt, c), F32), pltpu.VMEM((HALO_B + bt, c), F32),
                        pltpu.VMEM((bt, c), F32), pltpu.VMEM((bt, 2 * c), BF16)],
        compiler_params=_params(("parallel", "arbitrary")),
        name="conv_prompt",
    )(ua, ua, ub, ub, gb, x, dwa, ba, lng, lnb, dwb, w_out)


def _conv_s_body(sa_ref, ua_ref, sb_ref, ub_ref, gb_ref, x_ref, dwa_ref, ba_ref, lng_ref, lnb_ref, dwb_ref,
                 wo_ref, o_ref):
    ya = dwa_ref[KA - 1:KA, :] * ua_ref[...]
    for k in range(KA - 1):
        ya = ya + dwa_ref[k:k + 1, :] * sa_ref[k]
    yb = dwb_ref[KB - 1:KB, :] * ub_ref[...]
    for k in range(KB - 1):
        yb = yb + dwb_ref[k:k + 1, :] * sb_ref[k]
    ya = _ln_silu(ya + ba_ref[...], lng_ref[...], lnb_ref[...])
    yb = gb_ref[...] * yb
    yc = jnp.concatenate([ya.astype(BF16), yb.astype(BF16)], axis=1)
    o_ref[...] = x_ref[...] + _dot(yc, wo_ref[...])


def _conv_sample(sa_t, ua, sb_t, ub, gb, x, dwa, ba, lng, lnb, dwb, w_out):
    n, d = x.shape
    args = (sa_t, ua, sb_t, ub, gb, x, dwa, ba, lng, lnb, dwb, w_out)
    return pl.pallas_call(
        _conv_s_body,
        grid=(1,),
        in_specs=[_resident(a.shape) for a in args],
        out_specs=pl.BlockSpec((n, d), lambda i: (0, 0)),
        out_shape=jax.ShapeDtypeStruct((n, d), F32),
        compiler_params=_params(("arbitrary",)),
        name="conv_sample",
    )(*args)


def _mla_proj_body(x_ref, g_ref, cos_ref, sin_ref, wdq_ref, qn_ref, wqn_ref, wqr_ref, wqp_ref, gqn_ref, gqr_ref,
                   gqp_ref, gkn_ref, wkc_ref, kvn_ref, wkr_ref, wkp_ref, gkr_ref, gkp_ref, wuk_ref, wukt_ref,
                   sel_ref, selt_ref, ones_ref,
                   ckv_ref, kr_ref, ks_ref, ckvb_ref, krb_ref, ql_ref, qr_ref):
    nh = ql_ref.shape[0]
    h = _rms(x_ref[...], g_ref[...]).astype(BF16)
    cos = cos_ref[...]
    sin = sin_ref[...]
    sel = sel_ref[...]

    cq = _rms(_dot(h, wdq_ref[...]), qn_ref[...]).astype(BF16)
    qn = _dot(cq, wqn_ref[...])
    qr = _dot(cq, wqr_ref[...])
    qp = _dot(cq, wqp_ref[...])
    ssq = _hl_dot(qn * qn + qr * qr, sel)
    rq = _hl_dot(lax.rsqrt(ssq * (1.0 / QK_DIM) + EPS), selt_ref[...])
    cos_h = jnp.tile(cos, (1, nh))
    sin_h = jnp.tile(sin, (1, nh))
    qrot = (qr * rq * gqr_ref[...]) * cos_h + (qp * rq * gqp_ref[...]) * sin_h
    qnn = (qn * rq * gqn_ref[...]) * gkn_ref[...]
    for i in range(nh):
        sl = slice(i * LANES, (i + 1) * LANES)
        ql_ref[i] = _dot(qnn[:, sl].astype(BF16), wukt_ref[i]).astype(BF16)
        qr_ref[i] = qrot[:, sl].astype(BF16)

    ckv = _rms(_dot(h, wkc_ref[...]), kvn_ref[...])
    ckv_ref[...] = ckv
    ckvb = ckv.astype(BF16)
    ckvb_ref[...] = ckvb
    kraw = _dot(h, wkr_ref[...])
    kpar = _dot(h, wkp_ref[...])
    kn = _dot(ckvb, wuk_ref[...])
    ssk = _hl_dot(kn * kn, sel) + _hl_dot(kraw * kraw, ones_ref[...])
    ks_ref[...] = lax.rsqrt(ssk * (1.0 / QK_DIM) + EPS)
    kr = (kraw * gkr_ref[...]) * cos + (kpar * gkp_ref[...]) * sin
    kr_ref[...] = kr[:, 0:QK_ROPE]
    krb_ref[...] = kr.astype(BF16)


def _mla_proj(x, gain, cos_t, sin_t, weights, bm):
    n, d = x.shape
    nh = N_HEADS
    n_tab = cos_t.shape[0] // bm
    outs = [jax.ShapeDtypeStruct((n, KV_LORA), F32), jax.ShapeDtypeStruct((n, QK_ROPE), F32),
            jax.ShapeDtypeStruct((n, nh), F32), jax.ShapeDtypeStruct((n, KV_LORA), BF16),
            jax.ShapeDtypeStruct((n, LANES), BF16), jax.ShapeDtypeStruct((nh, n, KV_LORA), BF16),
            jax.ShapeDtypeStruct((nh, n, LANES), BF16)]

    def row(w):
        return pl.BlockSpec((bm, w), lambda m: (m, 0))

    tab = pl.BlockSpec((bm, LANES), lambda m: (m % n_tab, 0))
    return pl.pallas_call(
        _mla_proj_body,
        grid=(n // bm,),
        in_specs=[row(d), _resident(gain.shape), tab, tab] + [_resident(w.shape) for w in weights],
        out_specs=[row(KV_LORA), row(QK_ROPE), row(nh), row(KV_LORA), row(LANES),
                   pl.BlockSpec((nh, bm, KV_LORA), lambda m: (0, m, 0)),
                   pl.BlockSpec((nh, bm, LANES), lambda m: (0, m, 0))],
        out_shape=outs,
        compiler_params=_params(("parallel",)),
        name="mla_proj",
    )(x, gain, cos_t, sin_t, *weights)


def _attn_p_body(ql_ref, qr_ref, kc_ref, kr_ref, kst_ref, o_ref, m_ref, l_ref, acc_ref, *, bq, bk):
    nh = ql_ref.shape[0]
    qi = pl.program_id(1)
    ki = pl.program_id(2)

    @pl.when(ki == 0)
    def _():
        m_ref[...] = jnp.full(m_ref.shape, NEG, F32)
        l_ref[...] = jnp.zeros(l_ref.shape, F32)
        acc_ref[...] = jnp.zeros(acc_ref.shape, F32)

    @pl.when(ki <= qi)
    def _():
        kc = kc_ref[...]
        s = _dot_nt(ql_ref[...].reshape(nh * bq, KV_LORA), kc) + _dot_nt(qr_ref[...].reshape(nh * bq, LANES),
                                                                          kr_ref[...])
        kst = kst_ref[...] * ATTN_SCALE
        qpos = qi * bq + lax.broadcasted_iota(jnp.int32, (bq, bk), 0)
        kpos = ki * bk + lax.broadcasted_iota(jnp.int32, (bq, bk), 1)
        keep = kpos <= qpos
        for i in range(nh):
            rows = slice(i * bq, (i + 1) * bq)
            sh = jnp.where(keep, s[rows] * kst[i:i + 1, :], NEG)
            m_prev = m_ref[rows]
            m_new = jnp.maximum(m_prev, jnp.max(sh, axis=-1, keepdims=True))
            alpha = jnp.exp(m_prev - m_new)
            p = jnp.exp(sh - m_new)
            l_ref[rows] = alpha * l_ref[rows] + jnp.sum(p, axis=-1, keepdims=True)
            acc_ref[rows] = alpha * acc_ref[rows] + _dot(p.astype(BF16), kc)
            m_ref[rows] = m_new

    @pl.when(ki == qi)
    def _():
        for i in range(nh):
            rows = slice(i * bq, (i + 1) * bq)
            o_ref[i] = (acc_ref[rows] / l_ref[rows]).astype(BF16)


def _attn_prompt(ql, qr, kc, kr, kst, bq):
    nh, b, t, c = ql.shape
    bk = bq
    body = functools.partial(_attn_p_body, bq=bq, bk=bk)
    return pl.pallas_call(
        body,
        grid=(b, t // bq, t // bk),
        in_specs=[pl.BlockSpec((nh, None, bq, c), lambda i, q, k: (0, i, q, 0)),
                  pl.BlockSpec((nh, None, bq, LANES), lambda i, q, k: (0, i, q, 0)),
                  pl.BlockSpec((None, bk, c), lambda i, q, k: (i, jnp.minimum(k, q), 0)),
                  pl.BlockSpec((None, bk, LANES), lambda i, q, k: (i, jnp.minimum(k, q), 0)),
                  pl.BlockSpec((None, nh, bk), lambda i, q, k: (i, 0, jnp.minimum(k, q)))],
        out_specs=pl.BlockSpec((nh, None, bq, c), lambda i, q, k: (0, i, q, 0)),
        out_shape=jax.ShapeDtypeStruct((nh, b, t, c), BF16),
        scratch_shapes=[pltpu.VMEM((nh * bq, 1), F32), pltpu.VMEM((nh * bq, 1), F32),
                        pltpu.VMEM((nh * bq, c), F32)],
        compiler_params=_params(("parallel", "parallel", "arbitrary")),
        name="attn_prompt",
    )(ql, qr, kc, kr, kst)


def _attn_s_body(pt_ref, ql_ref, qr_ref, kcs_ref, krs_ref, kss_ref, cc_hbm, cr_hbm, cs_hbm, o_ref,
                 cbuf, rbuf, sbuf, kcb, krb, spad, sem, *, n_chunks):
    nh = ql_ref.shape[0]
    b = pl.program_id(0)
    nb = pl.num_programs(0)
    rows = PAGES_PER_CHUNK * PAGE_SIZE

    def copies(g, slot):
        seq = g // n_chunks
        base = (g % n_chunks) * PAGES_PER_CHUNK
        out = []
        for i in range(PAGES_PER_CHUNK):
            page = pt_ref[seq, base + i]
            dst = pl.ds(i * PAGE_SIZE, PAGE_SIZE)
            out.append(pltpu.make_async_copy(cc_hbm.at[page], cbuf.at[slot, dst], sem.at[0, slot]))
            out.append(pltpu.make_async_copy(cr_hbm.at[page], rbuf.at[slot, dst], sem.at[1, slot]))
            out.append(pltpu.make_async_copy(cs_hbm.at[page], sbuf.at[slot, dst], sem.at[2, slot]))
        return out

    @pl.when(b == 0)
    def _():
        for cp in copies(0, 0):
            cp.start()
        krb[...] = jnp.zeros(krb.shape, BF16)
        spad[...] = jnp.zeros(spad.shape, F32)

    ql = ql_ref[...]
    qr = qr_ref[...]

    kc_self = kcs_ref[...]
    s_self = (jnp.sum(ql.astype(F32) * kc_self.astype(F32), axis=-1, keepdims=True)
              + jnp.sum(qr.astype(F32) * krs_ref[...].astype(F32), axis=-1, keepdims=True))
    m0 = s_self * (kss_ref[...] * ATTN_SCALE)
    l0 = jnp.ones((nh, 1), F32)
    acc0 = jnp.broadcast_to(kc_self.astype(F32), (nh, KV_LORA))

    def step(c, carry):
        m_prev, l_prev, acc = carry
        g = b * n_chunks + c
        slot = g % 2

        @pl.when(g + 1 < nb * n_chunks)
        def _():
            for cp in copies(g + 1, 1 - slot):
                cp.start()

        for cp in copies(g, slot):
            cp.wait()

        kcb[...] = cbuf[slot].astype(BF16)
        krb[:, 0:QK_ROPE] = rbuf[slot].astype(BF16)
        spad[:, 0:nh] = sbuf[slot]
        kst = jnp.transpose(spad[...])[0:nh, :] * ATTN_SCALE
        kc = kcb[...]
        s = (_dot_nt(ql, kc) + _dot_nt(qr, krb[...])) * kst
        m_new = jnp.maximum(m_prev, jnp.max(s, axis=-1, keepdims=True))
        alpha = jnp.exp(m_prev - m_new)
        p = jnp.exp(s - m_new)
        l_new = alpha * l_prev + jnp.sum(p, axis=-1, keepdims=True)
        acc_new = alpha * acc + _dot(p.astype(BF16), kc)
        return m_new, l_new, acc_new

    _, l_fin, acc_fin = lax.fori_loop(0, n_chunks, step, (m0, l0, acc0))
    o_ref[...] = (acc_fin / l_fin).astype(BF16)


def _attn_sample(page_table, ql, qr, kc_self, kr_self, ks_self, cache_c, cache_r, cache_s):
    n, nh, c = ql.shape
    n_chunks = page_table.shape[1] // PAGES_PER_CHUNK
    rows = PAGES_PER_CHUNK * PAGE_SIZE
    body = functools.partial(_attn_s_body, n_chunks=n_chunks)
    any_spec = pl.BlockSpec(memory_space=pl.ANY)
    grid_spec = pltpu.PrefetchScalarGridSpec(
        num_scalar_prefetch=1,
        grid=(n,),
        in_specs=[pl.BlockSpec((None, nh, c), lambda i, pt: (i, 0, 0)),
                  pl.BlockSpec((None, nh, LANES), lambda i, pt: (i, 0, 0)),
                  pl.BlockSpec((None, 1, c), lambda i, pt: (i, 0, 0)),
                  pl.BlockSpec((None, 1, LANES), lambda i, pt: (i, 0, 0)),
                  pl.BlockSpec((None, nh, 1), lambda i, pt: (i, 0, 0)),
                  any_spec, any_spec, any_spec],
        out_specs=pl.BlockSpec((None, nh, c), lambda i, pt: (i, 0, 0)),
        scratch_shapes=[pltpu.VMEM((2, rows, c), F32), pltpu.VMEM((2, rows, QK_ROPE), F32),
                        pltpu.VMEM((2, rows, nh), F32), pltpu.VMEM((rows, c), BF16),
                        pltpu.VMEM((rows, LANES), BF16), pltpu.VMEM((rows, LANES), F32),
                        pltpu.SemaphoreType.DMA((3, 2))],
    )
    return pl.pallas_call(
        body,
        grid_spec=grid_spec,
        out_shape=jax.ShapeDtypeStruct((n, nh, c), BF16),
        compiler_params=_params(("arbitrary",)),
        name="attn_sample",
    )(page_table, ql, qr, kc_self, kr_self, ks_self, cache_c, cache_r, cache_s)


def _mla_out_body(ol_ref, x_ref, wuv_ref, wo_ref, o_ref, ov_ref):
    nh = ol_ref.shape[0]
    for i in range(nh):
        ov_ref[:, i * LANES:(i + 1) * LANES] = _dot(ol_ref[i], wuv_ref[i]).astype(BF16)
    o_ref[...] = x_ref[...] + _dot(ov_ref[...], wo_ref[...])


def _mla_out(o_lat, x, w_uv_h, w_o, bm):
    nh, n, c = o_lat.shape
    d = x.shape[1]
    return pl.pallas_call(
        _mla_out_body,
        grid=(n // bm,),
        in_specs=[pl.BlockSpec((nh, bm, c), lambda m: (0, m, 0)),
                  pl.BlockSpec((bm, d), lambda m: (m, 0)),
                  _resident(w_uv_h.shape), _resident(w_o.shape)],
        out_specs=pl.BlockSpec((bm, d), lambda m: (m, 0)),
        out_shape=jax.ShapeDtypeStruct((n, d), F32),
        scratch_shapes=[pltpu.VMEM((bm, nh * LANES), BF16)],
        compiler_params=_params(("parallel",)),
        name="mla_out",
    )(o_lat, x, w_uv_h, w_o)


def _rope_tables(pos):
    half = QK_ROPE // 2
    inv_freq = ROPE_THETA ** (-jnp.arange(half, dtype=F32) / half)
    ang = pos.astype(F32)[:, None] * inv_freq[None, :]
    cos, sin = jnp.cos(ang), jnp.sin(ang)
    z = jnp.zeros((pos.shape[0], LANES - QK_ROPE), F32)
    return jnp.concatenate([cos, cos, z], axis=1), jnp.concatenate([-sin, sin, z], axis=1)


def _swap_halves(a):
    half = QK_ROPE // 2
    return jnp.concatenate([a[..., half:], a[..., :half]], axis=-1)


def _pad_rope(a):
    return jnp.concatenate([a, jnp.zeros(a.shape[:-1] + (LANES - QK_ROPE,), a.dtype)], axis=-1)


def _mla_weights(w_dq, q_norm, w_uq, w_dkv, kv_norm, w_uk, gain_q, gain_k):
    nh = N_HEADS
    lq = w_uq.shape[0]
    w_rope = w_uq[:, :, QK_NOPE:]
    per_head = lambda a: a.reshape(lq, nh * LANES).astype(BF16)
    tile = lambda v: jnp.tile(v, nh)[None, :]
    head_of_lane = jnp.arange(nh * LANES) // LANES
    sel = (head_of_lane[:, None] == jnp.arange(nh)[None, :]).astype(BF16)
    w_kr = w_dkv[:, KV_LORA:]
    return [
        w_dq.astype(BF16), q_norm[None, :],
        per_head(w_uq[:, :, :QK_NOPE]), per_head(_pad_rope(w_rope)), per_head(_pad_rope(_swap_halves(w_rope))),
        tile(gain_q[:QK_NOPE]), tile(_pad_rope(gain_q[QK_NOPE:])), tile(_pad_rope(_swap_halves(gain_q[QK_NOPE:]))),
        tile(gain_k[:QK_NOPE]),
        w_dkv[:, :KV_LORA].astype(BF16), kv_norm[None, :],
        _pad_rope(w_kr).astype(BF16), _pad_rope(_swap_halves(w_kr)).astype(BF16),
        _pad_rope(gain_k[QK_NOPE:])[None, :], _pad_rope(_swap_halves(gain_k[QK_NOPE:]))[None, :],
        w_uk.reshape(w_uk.shape[0], nh * QK_NOPE).astype(BF16),
        jnp.transpose(w_uk, (1, 2, 0)).astype(BF16),
        sel, jnp.transpose(sel), jnp.ones((LANES, nh), BF16),
    ]


def _tile_rows(n, pref):
    return pref if n % pref == 0 else n


def kernel(x_prompt, x_sample, p_prompt, p_sample, state_conv_a, state_conv_b, cache_ckv, cache_krope,
           cache_kscale, page_table, norm_gains, ffn_w_gu, ffn_w_down, ple_w_gate, ple_w_proj, conv_w_in,
           conv_dw_a, conv_dw_a_bias, conv_ln_gain, conv_ln_bias, conv_dw_b, conv_w_out, mla_w_dq, mla_q_norm,
           mla_w_uq, mla_w_dkv, mla_kv_norm, mla_w_uk, mla_w_uv, mla_qk_gain_q, mla_qk_gain_k, mla_w_o):
    bsz, seq, d = x_prompt.shape
    dec = x_sample.shape[0]
    depth = norm_gains.shape[0]
    nh = N_HEADS
    past_len = page_table.shape[1] * PAGE_SIZE
    n_p = bsz * seq

    xp = x_prompt.reshape(n_p, d)
    xs = x_sample.reshape(dec, d)
    bm_p = _tile_rows(n_p, 512)
    bf = _tile_rows(ffn_w_down.shape[2], 512)

    cos_p, sin_p = _rope_tables(jnp.arange(seq, dtype=jnp.int32))
    cos_s, sin_s = _rope_tables(jnp.full((dec,), past_len, jnp.int32))

    ca_p, ca_s, cb_p, cb_s = [], [], [], []
    ckv_p, ckv_s, kr_p, kr_s, ks_p, ks_s = [], [], [], [], [], []
    for i in range(depth):
        j = i // 2
        gains = norm_gains[i][:, None, :]
        w_gu = ffn_w_gu[i].astype(BF16)
        w_down = ffn_w_down[i].astype(BF16)

        xp = _ffn(xp, gains[0], w_gu[0], w_down[0], bm_p, bf)
        xs = _ffn(xs, gains[0], w_gu[0], w_down[0], dec, bf)

        if i % 2 == 0:
            w_in = conv_w_in[j].astype(BF16)
            w_out = conv_w_out[j].astype(BF16)
            dwa = jnp.concatenate([conv_dw_a[j], jnp.zeros((1, CONV_WIDTH), F32)], axis=0)
            dwb = jnp.concatenate([conv_dw_b[j], jnp.zeros((8 - KB, CONV_WIDTH), F32)], axis=0)
            ba, lng, lnb = conv_dw_a_bias[j][None, :], conv_ln_gain[j][None, :], conv_ln_bias[j][None, :]

            ua, ub, gb = _conv_in(xp, gains[1], w_in, bm_p, 256)
            ua3, ub3 = ua.reshape(bsz, seq, CONV_WIDTH), ub.reshape(bsz, seq, CONV_WIDTH)
            xp = _conv_prompt(ua3, ub3, gb.reshape(bsz, seq, CONV_WIDTH), xp.reshape(bsz, seq, d),
                              dwa, ba, lng, lnb, dwb, w_out, _tile_rows(seq, 512)).reshape(n_p, d)
            ca_p.append(ua3[:, seq - (KA - 1):])
            cb_p.append(ub3[:, seq - (KB - 1):])

            ua, ub, gb = _conv_in(xs, gains[1], w_in, dec, 256)
            sa, sb = state_conv_a[j], state_conv_b[j]
            xs = _conv_sample(jnp.swapaxes(sa, 0, 1), ua, jnp.swapaxes(sb, 0, 1), ub, gb, xs,
                              dwa, ba, lng, lnb, dwb, w_out)
            ca_s.append(jnp.concatenate([sa[:, 1:], ua[:, None, :]], axis=1))
            cb_s.append(jnp.concatenate([sb[:, 1:], ub[:, None, :]], axis=1))
        else:
            weights = _mla_weights(mla_w_dq[j], mla_q_norm[j], mla_w_uq[j], mla_w_dkv[j], mla_kv_norm[j],
                                   mla_w_uk[j], mla_qk_gain_q[j], mla_qk_gain_k[j])
            w_uv_h = jnp.transpose(mla_w_uv[j], (1, 0, 2)).astype(BF16)
            w_o = mla_w_o[j].astype(BF16)

            bmq = _tile_rows(seq, 256)
            ckv, kr, ks, ckvb, krb, ql, qr = _mla_proj(xp, gains[1], cos_p, sin_p, weights, bmq)
            kst = jnp.swapaxes(ks.reshape(bsz, seq, nh), 1, 2)
            o_lat = _attn_prompt(ql.reshape(nh, bsz, seq, KV_LORA), qr.reshape(nh, bsz, seq, LANES),
                                 ckvb.reshape(bsz, seq, KV_LORA), krb.reshape(bsz, seq, LANES), kst, bmq)
            xp = _mla_out(o_lat.reshape(nh, n_p, KV_LORA), xp, w_uv_h, w_o, bm_p)
            npg = seq // PAGE_SIZE
            ckv_p.append(ckv.reshape(bsz, npg, PAGE_SIZE, KV_LORA))
            kr_p.append(kr.reshape(bsz, npg, PAGE_SIZE, QK_ROPE))
            ks_p.append(ks.reshape(bsz, npg, PAGE_SIZE, nh))

            ckv, kr, ks, ckvb, krb, ql, qr = _mla_proj(xs, gains[1], cos_s, sin_s, weights, dec)
            o_lat = _attn_sample(page_table, jnp.swapaxes(ql, 0, 1), jnp.swapaxes(qr, 0, 1), ckvb[:, None, :],
                                 krb[:, None, :], ks[:, :, None], cache_ckv[j], cache_krope[j], cache_kscale[j])
            xs = _mla_out(jnp.swapaxes(o_lat, 0, 1), xs, w_uv_h, w_o, dec)
            ckv_s.append(ckv[:, None, :])
            kr_s.append(kr[:, None, :])
            ks_s.append(ks[:, None, :])

        xp = _ffn(xp, gains[2], w_gu[1], w_down[1], bm_p, bf)
        xs = _ffn(xs, gains[2], w_gu[1], w_down[1], dec, bf)
        w_gate = ple_w_gate[i].astype(BF16)
        w_proj = ple_w_proj[i].astype(BF16)
        xp = _ple(xp, gains[3], w_gate, p_prompt[i].reshape(n_p, -1), w_proj, bm_p)
        xs = _ple(xs, gains[3], w_gate, p_sample[i].reshape(dec, -1), w_proj, dec)

    return (xp.reshape(bsz, seq, d), xs.reshape(dec, 1, d),
            jnp.stack(ca_p), jnp.stack(ca_s), jnp.stack(cb_p), jnp.stack(cb_s),
            jnp.stack(ckv_p), jnp.stack(ckv_s), jnp.stack(kr_p), jnp.stack(kr_s),
            jnp.stack(ks_p), jnp.stack(ks_s))
```

```python
import functools

import jax
import jax.numpy as jnp
from jax import lax
from jax.experimental import pallas as pl
from jax.experimental.pallas import tpu as pltpu

F32 = jnp.float32
BF16 = jnp.bfloat16

EPS = 1e-6
ROPE_THETA = 10000.0
NEG = -1e30
PAGE_SIZE = 128
N_HEADS = 16
QK_NOPE = 128
QK_ROPE = 64
QK_DIM = QK_NOPE + QK_ROPE
KV_LORA = 512
KA = 31
KB = 3
CONV_WIDTH = 1024
ATTN_SCALE = QK_DIM ** -0.5
LANES = 128
VMEM_LIMIT = 56 * 1024 * 1024
PAGES_PER_CHUNK = 8
CHUNK_SLOTS = 4


def _params(semantics):
    return pltpu.CompilerParams(dimension_semantics=semantics, vmem_limit_bytes=VMEM_LIMIT)


def _resident(shape):
    nd = len(shape)
    return pl.BlockSpec(shape, lambda *_: (0,) * nd, pipeline_mode=pl.Buffered(1))


def _rms(x, g):
    return x * lax.rsqrt(jnp.mean(x * x, axis=-1, keepdims=True) + EPS) * g


def _dot(a, b):
    return jnp.dot(a, b, preferred_element_type=F32)


def _dot_nt(a, b):
    return lax.dot_general(a, b, (((1,), (1,)), ((), ())), preferred_element_type=F32)


def _hl_dot(v, sel):
    hi = v.astype(BF16)
    lo = (v - hi.astype(F32)).astype(BF16)
    return _dot(hi, sel) + _dot(lo, sel)


def _ffn_body(x_ref, g_ref, wg_ref, wu_ref, wd_ref, o_ref, h_ref):
    @pl.when(pl.program_id(1) == 0)
    def _():
        x = x_ref[...]
        h_ref[...] = _rms(x, g_ref[...]).astype(BF16)
        o_ref[...] = x

    h = h_ref[...]
    g = _dot(h, wg_ref[...])
    u = _dot(h, wu_ref[...])
    a = (0.5 * g * jax.nn.sigmoid(g) * u).astype(BF16)
    o_ref[...] += _dot(a, wd_ref[...])


def _ffn(x, gain, w_gu, w_down, layer, which, bm, bf):
    n, d = x.shape
    nf = w_down.shape[2] // bf
    return pl.pallas_call(
        _ffn_body,
        grid=(n // bm, nf),
        in_specs=[pl.BlockSpec((bm, d), lambda m, f: (m, 0), pipeline_mode=pl.Buffered(1)),
                  pl.BlockSpec((1, d), lambda m, f: (0, 0)),
                  pl.BlockSpec((None, None, d, bf), lambda m, f: (layer, which, 0, f)),
                  pl.BlockSpec((None, None, d, bf), lambda m, f: (layer, which, 0, nf + f)),
                  pl.BlockSpec((None, None, bf, d), lambda m, f: (layer, which, f, 0))],
        out_specs=pl.BlockSpec((bm, d), lambda m, f: (m, 0)),
        out_shape=jax.ShapeDtypeStruct((n, d), F32),
        scratch_shapes=[pltpu.VMEM((bm, d), BF16)],
        compiler_params=_params(("parallel", "arbitrary")),
        name="ffn",
    )(x, gain, w_gu, w_gu, w_down)


def _layer_resident(shape, layer):
    nd = len(shape) - 1
    return pl.BlockSpec((None,) + tuple(shape[1:]), lambda *_: (layer,) + (0,) * nd, pipeline_mode=pl.Buffered(1))


def _ple_body(x_ref, g_ref, wg_ref, p_ref, wp_ref, o_ref):
    x = x_ref[...]
    h = _rms(x, g_ref[...]).astype(BF16)
    gate = jax.nn.sigmoid(_dot(h, wg_ref[...]))
    o_ref[...] = x + gate * _dot(p_ref[...].astype(BF16), wp_ref[...])


def _ple(x, gain, w_gate, p, w_proj, layer, bm):
    n, d = x.shape
    pd = p.shape[2]
    return pl.pallas_call(
        _ple_body,
        grid=(n // bm,),
        in_specs=[pl.BlockSpec((bm, d), lambda m: (m, 0)),
                  _resident((1, d)),
                  _layer_resident(w_gate.shape, layer),
                  pl.BlockSpec((None, bm, pd), lambda m: (layer, m, 0)),
                  _layer_resident(w_proj.shape, layer)],
        out_specs=pl.BlockSpec((bm, d), lambda m: (m, 0)),
        out_shape=jax.ShapeDtypeStruct((n, d), F32),
        compiler_params=_params(("parallel",)),
        name="ple",
    )(x, gain, w_gate, p, w_proj)


def _conv_in_body(x_ref, g_ref, wav_ref, wag_ref, wbh_ref, wbb_ref, wbc_ref, ua_ref, ub_ref, gb_ref, h_ref):
    @pl.when(pl.program_id(1) == 0)
    def _():
        h_ref[...] = _rms(x_ref[...], g_ref[...]).astype(BF16)

    h = h_ref[...]
    ua_ref[...] = _dot(h, wav_ref[...]) * jax.nn.sigmoid(_dot(h, wag_ref[...]))
    ub_ref[...] = _dot(h, wbc_ref[...]) * _dot(h, wbh_ref[...])
    gb_ref[...] = _dot(h, wbb_ref[...])


def _conv_in(x, gain, w_in, bm, bc):
    n, d = x.shape
    c = CONV_WIDTH
    nc = c // bc

    def wspec(k):
        return pl.BlockSpec((d, bc), lambda m, j: (0, k * nc + j))

    out = jax.ShapeDtypeStruct((n, c), F32)
    ospec = pl.BlockSpec((bm, bc), lambda m, j: (m, j))
    return pl.pallas_call(
        _conv_in_body,
        grid=(n // bm, nc),
        in_specs=[pl.BlockSpec((bm, d), lambda m, j: (m, 0)),
                  pl.BlockSpec((1, d), lambda m, j: (0, 0)),
                  wspec(0), wspec(1), wspec(2), wspec(3), wspec(4)],
        out_specs=[ospec, ospec, ospec],
        out_shape=[out, out, out],
        scratch_shapes=[pltpu.VMEM((bm, d), BF16)],
        compiler_params=_params(("parallel", "arbitrary")),
        name="conv_in",
    )(x, gain, w_in, w_in, w_in, w_in, w_in)


def _ln_silu(y, g, b):
    yc = y - jnp.mean(y, axis=-1, keepdims=True)
    z = yc * lax.rsqrt(jnp.mean(yc * yc, axis=-1, keepdims=True) + EPS) * g + b
    return z * jax.nn.sigmoid(z)


HALO_A = 32
HALO_B = 8
CONV_ROWS = 64


def _conv_p_body(ua_ref, uah_ref, ub_ref, ubh_ref, gb_ref, x_ref, dwa_ref, ba_ref, lng_ref, lnb_ref, dwb_ref,
                 wo_ref, o_ref, pa_ref, pb_ref, ya_ref, yc_ref, *, bt):
    first = pl.program_id(1) == 0
    pa_ref[0:HALO_A, :] = jnp.where(first, 0.0, uah_ref[...])
    pa_ref[HALO_A:HALO_A + bt, :] = ua_ref[...]
    pb_ref[0:HALO_B, :] = jnp.where(first, 0.0, ubh_ref[...])
    pb_ref[HALO_B:HALO_B + bt, :] = ub_ref[...]
    c = pa_ref.shape[1]

    def chunk(cc, carry):
        cs = pl.multiple_of(cc * LANES, LANES)
        wa = dwa_ref[:, pl.ds(cs, LANES)]
        wb = dwb_ref[:, pl.ds(cs, LANES)]
        for r in range(bt // CONV_ROWS):
            acc = jnp.zeros((CONV_ROWS, LANES), F32)
            for k in range(KA):
                acc = acc + wa[k:k + 1, :] * pa_ref[pl.ds(r * CONV_ROWS + HALO_A - (KA - 1) + k, CONV_ROWS),
                                                    pl.ds(cs, LANES)]
            ya_ref[pl.ds(r * CONV_ROWS, CONV_ROWS), pl.ds(cs, LANES)] = acc
            accb = jnp.zeros((CONV_ROWS, LANES), F32)
            for k in range(KB):
                accb = accb + wb[k:k + 1, :] * pb_ref[pl.ds(r * CONV_ROWS + HALO_B - (KB - 1) + k, CONV_ROWS),
                                                      pl.ds(cs, LANES)]
            yb = gb_ref[pl.ds(r * CONV_ROWS, CONV_ROWS), pl.ds(cs, LANES)] * accb
            yc_ref[pl.ds(r * CONV_ROWS, CONV_ROWS), pl.ds(c + cs, LANES)] = yb.astype(BF16)
        return carry

    lax.fori_loop(0, c // LANES, chunk, 0)
    ya = _ln_silu(ya_ref[...] + ba_ref[...], lng_ref[...], lnb_ref[...])
    yc_ref[:, 0:c] = ya.astype(BF16)
    o_ref[...] = x_ref[...] + _dot(yc_ref[...], wo_ref[...])


def _conv_prompt(ua, ub, gb, x, dwa, ba, lng, lnb, dwb, w_out, bt):
    b, t, c = ua.shape
    d = x.shape[2]
    body = functools.partial(_conv_p_body, bt=bt)
    cur = pl.BlockSpec((None, bt, c), lambda i, j: (i, j, 0))
    return pl.pallas_call(
        body,
        grid=(b, t // bt),
        in_specs=[cur,
                  pl.BlockSpec((None, HALO_A, c), lambda i, j: (i, jnp.maximum(j * (bt // HALO_A) - 1, 0), 0)),
                  cur,
                  pl.BlockSpec((None, HALO_B, c), lambda i, j: (i, jnp.maximum(j * (bt // HALO_B) - 1, 0), 0)),
                  cur,
                  pl.BlockSpec((None, bt, d), lambda i, j: (i, j, 0)),
                  _resident(dwa.shape), _resident(ba.shape), _resident(lng.shape), _resident(lnb.shape),
                  _resident(dwb.shape), _resident(w_out.shape)],
        out_specs=pl.BlockSpec((None, bt, d), lambda i, j: (i, j, 0)),
        out_shape=jax.ShapeDtypeStruct((b, t, d), F32),
        scratch_shapes=[pltpu.VMEM((HALO_A + bt, c), F32), pltpu.VMEM((HALO_B + bt, c), F32),
                        pltpu.VMEM((bt, c), F32), pltpu.VMEM((bt, 2 * c), BF16)],
        compiler_params=_params(("parallel", "arbitrary")),
        name="conv_prompt",
    )(ua, ua, ub, ub, gb, x, dwa, ba, lng, lnb, dwb, w_out)


def _conv_s_body(sa_ref, ua_ref, sb_ref, ub_ref, gb_ref, x_ref, dwa_ref, ba_ref, lng_ref, lnb_ref, dwb_ref,
                 wo_ref, o_ref):
    ya = dwa_ref[KA - 1:KA, :] * ua_ref[...]
    for k in range(KA - 1):
        ya = ya + dwa_ref[k:k + 1, :] * sa_ref[k]
    yb = dwb_ref[KB - 1:KB, :] * ub_ref[...]
    for k in range(KB - 1):
        yb = yb + dwb_ref[k:k + 1, :] * sb_ref[k]
    ya = _ln_silu(ya + ba_ref[...], lng_ref[...], lnb_ref[...])
    yb = gb_ref[...] * yb
    yc = jnp.concatenate([ya.astype(BF16), yb.astype(BF16)], axis=1)
    o_ref[...] = x_ref[...] + _dot(yc, wo_ref[...])


def _conv_sample(sa_t, ua, sb_t, ub, gb, x, dwa, ba, lng, lnb, dwb, w_out):
    n, d = x.shape
    args = (sa_t, ua, sb_t, ub, gb, x, dwa, ba, lng, lnb, dwb, w_out)
    return pl.pallas_call(
        _conv_s_body,
        grid=(1,),
        in_specs=[_resident(a.shape) for a in args],
        out_specs=pl.BlockSpec((n, d), lambda i: (0, 0)),
        out_shape=jax.ShapeDtypeStruct((n, d), F32),
        compiler_params=_params(("arbitrary",)),
        name="conv_sample",
    )(*args)


def _mla_proj_body(x_ref, g_ref, cos_ref, sin_ref, wdq_ref, qn_ref, wqn_ref, wqr_ref, wqp_ref, gqn_ref, gqr_ref,
                   gqp_ref, gkn_ref, wkc_ref, kvn_ref, wkr_ref, wkp_ref, gkr_ref, gkp_ref, wuk_ref, wukt_ref,
                   sel_ref, selt_ref, ones_ref,
                   ckv_ref, kr_ref, ks_ref, ckvb_ref, krb_ref, ql_ref, qr_ref):
    nh = ql_ref.shape[0]
    h = _rms(x_ref[...], g_ref[...]).astype(BF16)
    cos = cos_ref[...]
    sin = sin_ref[...]
    sel = sel_ref[...]

    cq = _rms(_dot(h, wdq_ref[...]), qn_ref[...]).astype(BF16)
    qn = _dot(cq, wqn_ref[...])
    qr = _dot(cq, wqr_ref[...])
    qp = _dot(cq, wqp_ref[...])
    ssq = _hl_dot(qn * qn + qr * qr, sel)
    rq = _hl_dot(lax.rsqrt(ssq * (1.0 / QK_DIM) + EPS), selt_ref[...])
    cos_h = jnp.tile(cos, (1, nh))
    sin_h = jnp.tile(sin, (1, nh))
    qrot = (qr * rq * gqr_ref[...]) * cos_h + (qp * rq * gqp_ref[...]) * sin_h
    qnn = (qn * rq * gqn_ref[...]) * gkn_ref[...]
    for i in range(nh):
        sl = slice(i * LANES, (i + 1) * LANES)
        ql_ref[i] = _dot(qnn[:, sl].astype(BF16), wukt_ref[i]).astype(BF16)
        qr_ref[i] = qrot[:, sl].astype(BF16)

    ckv = _rms(_dot(h, wkc_ref[...]), kvn_ref[...])
    ckv_ref[...] = ckv
    ckvb = ckv.astype(BF16)
    ckvb_ref[...] = ckvb
    kraw = _dot(h, wkr_ref[...])
    kpar = _dot(h, wkp_ref[...])
    kn = _dot(ckvb, wuk_ref[...])
    ssk = _hl_dot(kn * kn, sel) + _hl_dot(kraw * kraw, ones_ref[...])
    ks_ref[...] = lax.rsqrt(ssk * (1.0 / QK_DIM) + EPS)
    kr = (kraw * gkr_ref[...]) * cos + (kpar * gkp_ref[...]) * sin
    kr_ref[...] = kr[:, 0:QK_ROPE]
    krb_ref[...] = kr.astype(BF16)


def _mla_proj(x, gain, cos_t, sin_t, weights, bm):
    n, d = x.shape
    nh = N_HEADS
    n_tab = cos_t.shape[0] // bm
    outs = [jax.ShapeDtypeStruct((n, KV_LORA), F32), jax.ShapeDtypeStruct((n, QK_ROPE), F32),
            jax.ShapeDtypeStruct((n, nh), F32), jax.ShapeDtypeStruct((n, KV_LORA), BF16),
            jax.ShapeDtypeStruct((n, LANES), BF16), jax.ShapeDtypeStruct((nh, n, KV_LORA), BF16),
            jax.ShapeDtypeStruct((nh, n, LANES), BF16)]

    def row(w):
        return pl.BlockSpec((bm, w), lambda m: (m, 0))

    tab = pl.BlockSpec((bm, LANES), lambda m: (m % n_tab, 0))
    return pl.pallas_call(
        _mla_proj_body,
        grid=(n // bm,),
        in_specs=[row(d), _resident(gain.shape), tab, tab] + [_resident(w.shape) for w in weights],
        out_specs=[row(KV_LORA), row(QK_ROPE), row(nh), row(KV_LORA), row(LANES),
                   pl.BlockSpec((nh, bm, KV_LORA), lambda m: (0, m, 0)),
                   pl.BlockSpec((nh, bm, LANES), lambda m: (0, m, 0))],
        out_shape=outs,
        compiler_params=_params(("parallel",)),
        name="mla_proj",
    )(x, gain, cos_t, sin_t, *weights)


HEADS_PER_STEP = 4


def _attn_p_body(ql_ref, qr_ref, kc_ref, kr_ref, kst_ref, o_ref, m_ref, l_ref, acc_ref, s_ref, *, bq, bk):
    hg = pl.program_id(1)
    qi = pl.program_id(2)
    ki = pl.program_id(3)
    nhs = ql_ref.shape[0]

    @pl.when(ki == 0)
    def _():
        m_ref[...] = jnp.full(m_ref.shape, NEG, F32)
        l_ref[...] = jnp.zeros(l_ref.shape, F32)
        acc_ref[...] = jnp.zeros(acc_ref.shape, F32)

    def scores():
        kc = kc_ref[...]
        kr = kr_ref[...]
        for i in range(nhs):
            s = _dot_nt(ql_ref[i], kc) + _dot_nt(qr_ref[i], kr)
            s_ref[i * bq:(i + 1) * bq, :] = s * (kst_ref[pl.ds(hg * nhs + i, 1), :] * ATTN_SCALE)

    def update(s):
        m_prev = m_ref[...]
        m_new = jnp.maximum(m_prev, jnp.max(s, axis=-1, keepdims=True))
        alpha = jnp.exp(m_prev - m_new)
        p = jnp.exp(s - m_new)
        l_ref[...] = alpha * l_ref[...] + jnp.sum(p, axis=-1, keepdims=True)
        acc_ref[...] = alpha * acc_ref[...] + _dot(p.astype(BF16), kc_ref[...])
        m_ref[...] = m_new

    @pl.when(ki < qi)
    def _():
        scores()
        update(s_ref[...])

    @pl.when(ki == qi)
    def _():
        scores()
        keep = lax.broadcasted_iota(jnp.int32, (bq, bk), 1) <= lax.broadcasted_iota(jnp.int32, (bq, bk), 0)
        update(jnp.where(jnp.tile(keep, (nhs, 1)), s_ref[...], NEG))
        o = acc_ref[...] / l_ref[...]
        for i in range(nhs):
            o_ref[i] = o[i * bq:(i + 1) * bq, :].astype(BF16)


def _attn_prompt(ql, qr, kc, kr, kst, bq):
    nh, b, t, c = ql.shape
    bk = bq
    nhs = HEADS_PER_STEP
    body = functools.partial(_attn_p_body, bq=bq, bk=bk)
    return pl.pallas_call(
        body,
        grid=(b, nh // nhs, t // bq, t // bk),
        in_specs=[pl.BlockSpec((nhs, None, bq, c), lambda i, h, q, k: (h, i, q, 0)),
                  pl.BlockSpec((nhs, None, bq, LANES), lambda i, h, q, k: (h, i, q, 0)),
                  pl.BlockSpec((None, bk, c), lambda i, h, q, k: (i, jnp.minimum(k, q), 0)),
                  pl.BlockSpec((None, bk, LANES), lambda i, h, q, k: (i, jnp.minimum(k, q), 0)),
                  pl.BlockSpec((None, nh, bk), lambda i, h, q, k: (i, 0, jnp.minimum(k, q)))],
        out_specs=pl.BlockSpec((nhs, None, bq, c), lambda i, h, q, k: (h, i, q, 0)),
        out_shape=jax.ShapeDtypeStruct((nh, b, t, c), BF16),
        scratch_shapes=[pltpu.VMEM((nhs * bq, 1), F32), pltpu.VMEM((nhs * bq, 1), F32),
                        pltpu.VMEM((nhs * bq, c), F32), pltpu.VMEM((nhs * bq, bk), F32)],
        compiler_params=_params(("parallel", "parallel", "parallel", "arbitrary")),
        name="attn_prompt",
    )(ql, qr, kc, kr, kst)


def _attn_s_body(pt_ref, ql_ref, qr_ref, kcs_ref, krs_ref, kss_ref, cc_hbm, crt_hbm, cst_hbm, o_ref,
                 cbuf, rbuf, sbuf, kcb, sem, *, n_chunks):
    nh = ql_ref.shape[0]
    b = pl.program_id(0)
    total = pl.num_programs(0) * n_chunks

    def copies(g, slot):
        seq = g // n_chunks
        base = (g % n_chunks) * PAGES_PER_CHUNK
        out = []
        for i in range(PAGES_PER_CHUNK):
            page = pt_ref[seq, base + i]
            keys = pl.ds(i * PAGE_SIZE, PAGE_SIZE)
            out.append(pltpu.make_async_copy(cc_hbm.at[page], cbuf.at[slot, keys], sem.at[0, slot]))
            out.append(pltpu.make_async_copy(crt_hbm.at[page], rbuf.at[slot, :, keys], sem.at[1, slot]))
            out.append(pltpu.make_async_copy(cst_hbm.at[page], sbuf.at[slot, :, keys], sem.at[2, slot]))
        return out

    @pl.when(b == 0)
    def _():
        for g0 in range(CHUNK_SLOTS - 1):
            for cp in copies(g0, g0):
                cp.start()

    ql = ql_ref[...]
    qr = qr_ref[...]
    qr_rope = qr[:, 0:QK_ROPE]

    kc_self = kcs_ref[...]
    s_self = (jnp.sum(ql.astype(F32) * kc_self.astype(F32), axis=-1, keepdims=True)
              + jnp.sum(qr.astype(F32) * krs_ref[...].astype(F32), axis=-1, keepdims=True))
    m0 = s_self * (kss_ref[...] * ATTN_SCALE)
    l0 = jnp.ones((nh, 1), F32)
    acc0 = jnp.broadcast_to(kc_self.astype(F32), (nh, KV_LORA))

    def step(c, carry):
        m_prev, l_prev, acc = carry
        g = b * n_chunks + c
        slot = g % CHUNK_SLOTS
        ahead = g + (CHUNK_SLOTS - 1)

        @pl.when(ahead < total)
        def _():
            for cp in copies(ahead, ahead % CHUNK_SLOTS):
                cp.start()

        for cp in copies(g, slot):
            cp.wait()

        kcb[...] = cbuf[slot].astype(BF16)
        kc = kcb[...]
        s = (_dot_nt(ql, kc) + _dot(qr_rope, rbuf[slot].astype(BF16))) * (sbuf[slot] * ATTN_SCALE)
        m_new = jnp.maximum(m_prev, jnp.max(s, axis=-1, keepdims=True))
        alpha = jnp.exp(m_prev - m_new)
        p = jnp.exp(s - m_new)
        l_new = alpha * l_prev + jnp.sum(p, axis=-1, keepdims=True)
        acc_new = alpha * acc + _dot(p.astype(BF16), kc)
        return m_new, l_new, acc_new

    _, l_fin, acc_fin = lax.fori_loop(0, n_chunks, step, (m0, l0, acc0))
    o_ref[...] = (acc_fin / l_fin).astype(BF16)


def _attn_sample(page_table, ql, qr, kc_self, kr_self, ks_self, cache_c, cache_r, cache_s):
    n, nh, c = ql.shape
    n_chunks = page_table.shape[1] // PAGES_PER_CHUNK
    assert n * n_chunks >= CHUNK_SLOTS - 1
    rows = PAGES_PER_CHUNK * PAGE_SIZE
    body = functools.partial(_attn_s_body, n_chunks=n_chunks)
    any_spec = pl.BlockSpec(memory_space=pl.ANY)
    grid_spec = pltpu.PrefetchScalarGridSpec(
        num_scalar_prefetch=1,
        grid=(n,),
        in_specs=[pl.BlockSpec((None, nh, c), lambda i, pt: (i, 0, 0)),
                  pl.BlockSpec((None, nh, LANES), lambda i, pt: (i, 0, 0)),
                  pl.BlockSpec((None, 1, c), lambda i, pt: (i, 0, 0)),
                  pl.BlockSpec((None, 1, LANES), lambda i, pt: (i, 0, 0)),
                  pl.BlockSpec((None, nh, 1), lambda i, pt: (i, 0, 0)),
                  any_spec, any_spec, any_spec],
        out_specs=pl.BlockSpec((None, nh, c), lambda i, pt: (i, 0, 0)),
        scratch_shapes=[pltpu.VMEM((CHUNK_SLOTS, rows, c), F32), pltpu.VMEM((CHUNK_SLOTS, QK_ROPE, rows), F32),
                        pltpu.VMEM((CHUNK_SLOTS, nh, rows), F32), pltpu.VMEM((rows, c), BF16),
                        pltpu.SemaphoreType.DMA((3, CHUNK_SLOTS))],
    )
    return pl.pallas_call(
        body,
        grid_spec=grid_spec,
        out_shape=jax.ShapeDtypeStruct((n, nh, c), BF16),
        compiler_params=_params(("arbitrary",)),
        name="attn_sample",
    )(page_table, ql, qr, kc_self, kr_self, ks_self, cache_c, cache_r, cache_s)


def _mla_out_body(ol_ref, x_ref, wuv_ref, wo_ref, o_ref, ov_ref):
    nh = ol_ref.shape[0]
    for i in range(nh):
        ov_ref[:, i * LANES:(i + 1) * LANES] = _dot(ol_ref[i], wuv_ref[i]).astype(BF16)
    o_ref[...] = x_ref[...] + _dot(ov_ref[...], wo_ref[...])


def _mla_out(o_lat, x, w_uv_h, w_o, bm):
    nh, n, c = o_lat.shape
    d = x.shape[1]
    return pl.pallas_call(
        _mla_out_body,
        grid=(n // bm,),
        in_specs=[pl.BlockSpec((nh, bm, c), lambda m: (0, m, 0)),
                  pl.BlockSpec((bm, d), lambda m: (m, 0)),
                  _resident(w_uv_h.shape), _resident(w_o.shape)],
        out_specs=pl.BlockSpec((bm, d), lambda m: (m, 0)),
        out_shape=jax.ShapeDtypeStruct((n, d), F32),
        scratch_shapes=[pltpu.VMEM((bm, nh * LANES), BF16)],
        compiler_params=_params(("parallel",)),
        name="mla_out",
    )(o_lat, x, w_uv_h, w_o)


def _rope_tables(pos):
    half = QK_ROPE // 2
    inv_freq = ROPE_THETA ** (-jnp.arange(half, dtype=F32) / half)
    ang = pos.astype(F32)[:, None] * inv_freq[None, :]
    cos, sin = jnp.cos(ang), jnp.sin(ang)
    z = jnp.zeros((pos.shape[0], LANES - QK_ROPE), F32)
    return jnp.concatenate([cos, cos, z], axis=1), jnp.concatenate([-sin, sin, z], axis=1)


def _swap_halves(a):
    half = QK_ROPE // 2
    return jnp.concatenate([a[..., half:], a[..., :half]], axis=-1)


def _pad_rope(a):
    return jnp.concatenate([a, jnp.zeros(a.shape[:-1] + (LANES - QK_ROPE,), a.dtype)], axis=-1)


def _mla_weights(w_dq, q_norm, w_uq, w_dkv, kv_norm, w_uk, gain_q, gain_k):
    nh = N_HEADS
    lq = w_uq.shape[0]
    w_rope = w_uq[:, :, QK_NOPE:]
    per_head = lambda a: a.reshape(lq, nh * LANES).astype(BF16)
    tile = lambda v: jnp.tile(v, nh)[None, :]
    head_of_lane = jnp.arange(nh * LANES) // LANES
    sel = (head_of_lane[:, None] == jnp.arange(nh)[None, :]).astype(BF16)
    w_kr = w_dkv[:, KV_LORA:]
    return [
        w_dq.astype(BF16), q_norm[None, :],
        per_head(w_uq[:, :, :QK_NOPE]), per_head(_pad_rope(w_rope)), per_head(_pad_rope(_swap_halves(w_rope))),
        tile(gain_q[:QK_NOPE]), tile(_pad_rope(gain_q[QK_NOPE:])), tile(_pad_rope(_swap_halves(gain_q[QK_NOPE:]))),
        tile(gain_k[:QK_NOPE]),
        w_dkv[:, :KV_LORA].astype(BF16), kv_norm[None, :],
        _pad_rope(w_kr).astype(BF16), _pad_rope(_swap_halves(w_kr)).astype(BF16),
        _pad_rope(gain_k[QK_NOPE:])[None, :], _pad_rope(_swap_halves(gain_k[QK_NOPE:]))[None, :],
        w_uk.reshape(w_uk.shape[0], nh * QK_NOPE).astype(BF16),
        jnp.transpose(w_uk, (1, 2, 0)).astype(BF16),
        sel, jnp.transpose(sel), jnp.ones((LANES, nh), BF16),
    ]


def _tile_rows(n, pref):
    return pref if n % pref == 0 else n


def kernel(x_prompt, x_sample, p_prompt, p_sample, state_conv_a, state_conv_b, cache_ckv, cache_krope,
           cache_kscale, page_table, norm_gains, ffn_w_gu, ffn_w_down, ple_w_gate, ple_w_proj, conv_w_in,
           conv_dw_a, conv_dw_a_bias, conv_ln_gain, conv_ln_bias, conv_dw_b, conv_w_out, mla_w_dq, mla_q_norm,
           mla_w_uq, mla_w_dkv, mla_kv_norm, mla_w_uk, mla_w_uv, mla_qk_gain_q, mla_qk_gain_k, mla_w_o):
    bsz, seq, d = x_prompt.shape
    dec = x_sample.shape[0]
    depth = norm_gains.shape[0]
    nh = N_HEADS
    past_len = page_table.shape[1] * PAGE_SIZE
    n_p = bsz * seq

    xp = x_prompt.reshape(n_p, d)
    xs = x_sample.reshape(dec, d)
    bm_p = _tile_rows(n_p, 512)
    bm_ffn = _tile_rows(n_p, 1024)
    bf = _tile_rows(ffn_w_down.shape[2], 512)
    w_gu = ffn_w_gu.astype(BF16)
    w_down = ffn_w_down.astype(BF16)
    w_gate = ple_w_gate.astype(BF16)
    w_proj = ple_w_proj.astype(BF16)
    pp = p_prompt.reshape(depth, n_p, -1)
    ps = p_sample.reshape(depth, dec, -1)

    cos_p, sin_p = _rope_tables(jnp.arange(seq, dtype=jnp.int32))
    cos_s, sin_s = _rope_tables(jnp.full((dec,), past_len, jnp.int32))

    ca_p, ca_s, cb_p, cb_s = [], [], [], []
    ckv_p, ckv_s, kr_p, kr_s, ks_p, ks_s = [], [], [], [], [], []
    for i in range(depth):
        j = i // 2
        gains = norm_gains[i][:, None, :]

        xp = _ffn(xp, gains[0], w_gu, w_down, i, 0, bm_ffn, bf)
        xs = _ffn(xs, gains[0], w_gu, w_down, i, 0, dec, bf)

        if i % 2 == 0:
            w_in = conv_w_in[j].astype(BF16)
            w_out = conv_w_out[j].astype(BF16)
            dwa = jnp.concatenate([conv_dw_a[j], jnp.zeros((1, CONV_WIDTH), F32)], axis=0)
            dwb = jnp.concatenate([conv_dw_b[j], jnp.zeros((8 - KB, CONV_WIDTH), F32)], axis=0)
            ba, lng, lnb = conv_dw_a_bias[j][None, :], conv_ln_gain[j][None, :], conv_ln_bias[j][None, :]

            ua, ub, gb = _conv_in(xp, gains[1], w_in, bm_p, 256)
            ua3, ub3 = ua.reshape(bsz, seq, CONV_WIDTH), ub.reshape(bsz, seq, CONV_WIDTH)
            xp = _conv_prompt(ua3, ub3, gb.reshape(bsz, seq, CONV_WIDTH), xp.reshape(bsz, seq, d),
                              dwa, ba, lng, lnb, dwb, w_out, _tile_rows(seq, 512)).reshape(n_p, d)
            ca_p.append(ua3[:, seq - (KA - 1):])
            cb_p.append(ub3[:, seq - (KB - 1):])

            ua, ub, gb = _conv_in(xs, gains[1], w_in, dec, 256)
            sa, sb = state_conv_a[j], state_conv_b[j]
            xs = _conv_sample(jnp.swapaxes(sa, 0, 1), ua, jnp.swapaxes(sb, 0, 1), ub, gb, xs,
                              dwa, ba, lng, lnb, dwb, w_out)
            ca_s.append(jnp.concatenate([sa[:, 1:], ua[:, None, :]], axis=1))
            cb_s.append(jnp.concatenate([sb[:, 1:], ub[:, None, :]], axis=1))
        else:
            weights = _mla_weights(mla_w_dq[j], mla_q_norm[j], mla_w_uq[j], mla_w_dkv[j], mla_kv_norm[j],
                                   mla_w_uk[j], mla_qk_gain_q[j], mla_qk_gain_k[j])
            w_uv_h = jnp.transpose(mla_w_uv[j], (1, 0, 2)).astype(BF16)
            w_o = mla_w_o[j].astype(BF16)

            bmq = _tile_rows(seq, 256)
            ckv, kr, ks, ckvb, krb, ql, qr = _mla_proj(xp, gains[1], cos_p, sin_p, weights, bmq)
            kst = jnp.swapaxes(ks.reshape(bsz, seq, nh), 1, 2)
            o_lat = _attn_prompt(ql.reshape(nh, bsz, seq, KV_LORA), qr.reshape(nh, bsz, seq, LANES),
                                 ckvb.reshape(bsz, seq, KV_LORA), krb.reshape(bsz, seq, LANES), kst,
                                 _tile_rows(seq, 512))
            xp = _mla_out(o_lat.reshape(nh, n_p, KV_LORA), xp, w_uv_h, w_o, bm_p)
            npg = seq // PAGE_SIZE
            ckv_p.append(ckv.reshape(bsz, npg, PAGE_SIZE, KV_LORA))
            kr_p.append(kr.reshape(bsz, npg, PAGE_SIZE, QK_ROPE))
            ks_p.append(ks.reshape(bsz, npg, PAGE_SIZE, nh))

            ckv, kr, ks, ckvb, krb, ql, qr = _mla_proj(xs, gains[1], cos_s, sin_s, weights, dec)
            o_lat = _attn_sample(page_table, jnp.swapaxes(ql, 0, 1), jnp.swapaxes(qr, 0, 1), ckvb[:, None, :],
                                 krb[:, None, :], ks[:, :, None], cache_ckv[j],
                                 jnp.swapaxes(cache_krope[j], 1, 2), jnp.swapaxes(cache_kscale[j], 1, 2))
            xs = _mla_out(jnp.swapaxes(o_lat, 0, 1), xs, w_uv_h, w_o, dec)
            ckv_s.append(ckv[:, None, :])
            kr_s.append(kr[:, None, :])
            ks_s.append(ks[:, None, :])

        xp = _ffn(xp, gains[2], w_gu, w_down, i, 1, bm_ffn, bf)
        xs = _ffn(xs, gains[2], w_gu, w_down, i, 1, dec, bf)
        xp = _ple(xp, gains[3], w_gate, pp, w_proj, i, bm_p)
        xs = _ple(xs, gains[3], w_gate, ps, w_proj, i, dec)

    return (xp.reshape(bsz, seq, d), xs.reshape(dec, 1, d),
            jnp.stack(ca_p), jnp.stack(ca_s), jnp.stack(cb_p), jnp.stack(cb_s),
            jnp.stack(ckv_p), jnp.stack(ckv_s), jnp.stack(kr_p), jnp.stack(kr_s),
            jnp.stack(ks_p), jnp.stack(ks_s))
```

```python
import functools

import jax
import jax.numpy as jnp
from jax import lax
from jax.experimental import pallas as pl
from jax.experimental.pallas import tpu as pltpu

F32 = jnp.float32
BF16 = jnp.bfloat16

EPS = 1e-6
ROPE_THETA = 10000.0
NEG = -1e30
PAGE_SIZE = 128
N_HEADS = 16
QK_NOPE = 128
QK_ROPE = 64
QK_DIM = QK_NOPE + QK_ROPE
KV_LORA = 512
KA = 31
KB = 3
CONV_WIDTH = 1024
ATTN_SCALE = QK_DIM ** -0.5
LOG2E = 1.4426950408889634
LANES = 128
VMEM_LIMIT = 56 * 1024 * 1024
PAGES_PER_CHUNK = 32
CHUNK_SLOTS = 3


def _params(semantics):
    return pltpu.CompilerParams(dimension_semantics=semantics, vmem_limit_bytes=VMEM_LIMIT)


def _resident(shape):
    nd = len(shape)
    return pl.BlockSpec(shape, lambda *_: (0,) * nd, pipeline_mode=pl.Buffered(1))


def _rms(x, g):
    return x * lax.rsqrt(jnp.mean(x * x, axis=-1, keepdims=True) + EPS) * g


def _dot(a, b):
    return jnp.dot(a, b, preferred_element_type=F32)


def _dot_nt(a, b):
    return lax.dot_general(a, b, (((1,), (1,)), ((), ())), preferred_element_type=F32)


def _hl_dot(v, sel):
    hi = v.astype(BF16)
    lo = (v - hi.astype(F32)).astype(BF16)
    return _dot(hi, sel) + _dot(lo, sel)


def _ffn_body(x_ref, g_ref, wg_ref, wu_ref, wd_ref, o_ref, h_ref):
    @pl.when(pl.program_id(1) == 0)
    def _():
        x = x_ref[...]
        h_ref[...] = _rms(x, g_ref[...]).astype(BF16)
        o_ref[...] = x

    h = h_ref[...]
    g = _dot(h, wg_ref[...])
    u = _dot(h, wu_ref[...])
    a = (0.5 * g * jax.nn.sigmoid(g) * u).astype(BF16)
    o_ref[...] += _dot(a, wd_ref[...])


def _ffn(x, gain, w_gu, w_down, layer, which, bm, bf):
    n, d = x.shape
    nf = w_down.shape[2] // bf
    return pl.pallas_call(
        _ffn_body,
        grid=(n // bm, nf),
        in_specs=[pl.BlockSpec((bm, d), lambda m, f: (m, 0), pipeline_mode=pl.Buffered(1)),
                  pl.BlockSpec((1, d), lambda m, f: (0, 0)),
                  pl.BlockSpec((None, None, d, bf), lambda m, f: (layer, which, 0, f)),
                  pl.BlockSpec((None, None, d, bf), lambda m, f: (layer, which, 0, nf + f)),
                  pl.BlockSpec((None, None, bf, d), lambda m, f: (layer, which, f, 0))],
        out_specs=pl.BlockSpec((bm, d), lambda m, f: (m, 0)),
        out_shape=jax.ShapeDtypeStruct((n, d), F32),
        scratch_shapes=[pltpu.VMEM((bm, d), BF16)],
        compiler_params=_params(("parallel", "arbitrary")),
        name="ffn",
    )(x, gain, w_gu, w_gu, w_down)


def _layer_resident(shape, layer):
    nd = len(shape) - 1
    return pl.BlockSpec((None,) + tuple(shape[1:]), lambda *_: (layer,) + (0,) * nd, pipeline_mode=pl.Buffered(1))


def _ple_body(x_ref, g_ref, wg_ref, p_ref, wp_ref, o_ref):
    x = x_ref[...]
    h = _rms(x, g_ref[...]).astype(BF16)
    gate = jax.nn.sigmoid(_dot(h, wg_ref[...]))
    o_ref[...] = x + gate * _dot(p_ref[...].astype(BF16), wp_ref[...])


def _ple(x, gain, w_gate, p, w_proj, layer, bm):
    n, d = x.shape
    pd = p.shape[2]
    return pl.pallas_call(
        _ple_body,
        grid=(n // bm,),
        in_specs=[pl.BlockSpec((bm, d), lambda m: (m, 0)),
                  _resident((1, d)),
                  _layer_resident(w_gate.shape, layer),
                  pl.BlockSpec((None, bm, pd), lambda m: (layer, m, 0)),
                  _layer_resident(w_proj.shape, layer)],
        out_specs=pl.BlockSpec((bm, d), lambda m: (m, 0)),
        out_shape=jax.ShapeDtypeStruct((n, d), F32),
        compiler_params=_params(("parallel",)),
        name="ple",
    )(x, gain, w_gate, p, w_proj)


def _conv_in_body(x_ref, g_ref, wav_ref, wag_ref, wbh_ref, wbb_ref, wbc_ref, ua_ref, ub_ref, gb_ref, h_ref):
    @pl.when(pl.program_id(1) == 0)
    def _():
        h_ref[...] = _rms(x_ref[...], g_ref[...]).astype(BF16)

    h = h_ref[...]
    ua_ref[...] = _dot(h, wav_ref[...]) * jax.nn.sigmoid(_dot(h, wag_ref[...]))
    ub_ref[...] = _dot(h, wbc_ref[...]) * _dot(h, wbh_ref[...])
    gb_ref[...] = _dot(h, wbb_ref[...])


def _conv_in(x, gain, w_in, bm, bc):
    n, d = x.shape
    c = CONV_WIDTH
    nc = c // bc

    def wspec(k):
        return pl.BlockSpec((d, bc), lambda m, j: (0, k * nc + j))

    out = jax.ShapeDtypeStruct((n, c), F32)
    ospec = pl.BlockSpec((bm, bc), lambda m, j: (m, j))
    return pl.pallas_call(
        _conv_in_body,
        grid=(n // bm, nc),
        in_specs=[pl.BlockSpec((bm, d), lambda m, j: (m, 0)),
                  pl.BlockSpec((1, d), lambda m, j: (0, 0)),
                  wspec(0), wspec(1), wspec(2), wspec(3), wspec(4)],
        out_specs=[ospec, ospec, ospec],
        out_shape=[out, out, out],
        scratch_shapes=[pltpu.VMEM((bm, d), BF16)],
        compiler_params=_params(("parallel", "arbitrary")),
        name="conv_in",
    )(x, gain, w_in, w_in, w_in, w_in, w_in)


def _ln_silu(y, g, b):
    yc = y - jnp.mean(y, axis=-1, keepdims=True)
    z = yc * lax.rsqrt(jnp.mean(yc * yc, axis=-1, keepdims=True) + EPS) * g + b
    return z * jax.nn.sigmoid(z)


HALO_A = 32
HALO_B = 8
CONV_ROWS = 64


def _conv_p_body(ua_ref, uah_ref, ub_ref, ubh_ref, gb_ref, x_ref, dwa_ref, ba_ref, lng_ref, lnb_ref, dwb_ref,
                 wo_ref, o_ref, pa_ref, pb_ref, ya_ref, yc_ref, *, bt):
    first = pl.program_id(1) == 0
    pa_ref[0:HALO_A, :] = jnp.where(first, 0.0, uah_ref[...])
    pa_ref[HALO_A:HALO_A + bt, :] = ua_ref[...]
    pb_ref[0:HALO_B, :] = jnp.where(first, 0.0, ubh_ref[...])
    pb_ref[HALO_B:HALO_B + bt, :] = ub_ref[...]
    c = pa_ref.shape[1]

    def chunk(cc, carry):
        cs = pl.multiple_of(cc * LANES, LANES)
        wa = dwa_ref[:, pl.ds(cs, LANES)]
        wb = dwb_ref[:, pl.ds(cs, LANES)]
        for r in range(bt // CONV_ROWS):
            acc = jnp.zeros((CONV_ROWS, LANES), F32)
            for k in range(KA):
                acc = acc + wa[k:k + 1, :] * pa_ref[pl.ds(r * CONV_ROWS + HALO_A - (KA - 1) + k, CONV_ROWS),
                                                    pl.ds(cs, LANES)]
            ya_ref[pl.ds(r * CONV_ROWS, CONV_ROWS), pl.ds(cs, LANES)] = acc
            accb = jnp.zeros((CONV_ROWS, LANES), F32)
            for k in range(KB):
                accb = accb + wb[k:k + 1, :] * pb_ref[pl.ds(r * CONV_ROWS + HALO_B - (KB - 1) + k, CONV_ROWS),
                                                      pl.ds(cs, LANES)]
            yb = gb_ref[pl.ds(r * CONV_ROWS, CONV_ROWS), pl.ds(cs, LANES)] * accb
            yc_ref[pl.ds(r * CONV_ROWS, CONV_ROWS), pl.ds(c + cs, LANES)] = yb.astype(BF16)
        return carry

    lax.fori_loop(0, c // LANES, chunk, 0)
    ya = _ln_silu(ya_ref[...] + ba_ref[...], lng_ref[...], lnb_ref[...])
    yc_ref[:, 0:c] = ya.astype(BF16)
    o_ref[...] = x_ref[...] + _dot(yc_ref[...], wo_ref[...])


def _conv_prompt(ua, ub, gb, x, dwa, ba, lng, lnb, dwb, w_out, bt):
    b, t, c = ua.shape
    d = x.shape[2]
    body = functools.partial(_conv_p_body, bt=bt)
    cur = pl.BlockSpec((None, bt, c), lambda i, j: (i, j, 0))
    return pl.pallas_call(
        body,
        grid=(b, t // bt),
        in_specs=[cur,
                  pl.BlockSpec((None, HALO_A, c), lambda i, j: (i, jnp.maximum(j * (bt // HALO_A) - 1, 0), 0)),
                  cur,
                  pl.BlockSpec((None, HALO_B, c), lambda i, j: (i, jnp.maximum(j * (bt // HALO_B) - 1, 0), 0)),
                  cur,
                  pl.BlockSpec((None, bt, d), lambda i, j: (i, j, 0)),
                  _resident(dwa.shape), _resident(ba.shape), _resident(lng.shape), _resident(lnb.shape),
                  _resident(dwb.shape), _resident(w_out.shape)],
        out_specs=pl.BlockSpec((None, bt, d), lambda i, j: (i, j, 0)),
        out_shape=jax.ShapeDtypeStruct((b, t, d), F32),
        scratch_shapes=[pltpu.VMEM((HALO_A + bt, c), F32), pltpu.VMEM((HALO_B + bt, c), F32),
                        pltpu.VMEM((bt, c), F32), pltpu.VMEM((bt, 2 * c), BF16)],
        compiler_params=_params(("parallel", "arbitrary")),
        name="conv_prompt",
    )(ua, ua, ub, ub, gb, x, dwa, ba, lng, lnb, dwb, w_out)


def _conv_s_body(sa_ref, ua_ref, sb_ref, ub_ref, gb_ref, x_ref, dwa_ref, ba_ref, lng_ref, lnb_ref, dwb_ref,
                 wo_ref, o_ref):
    ya = dwa_ref[KA - 1:KA, :] * ua_ref[...]
    for k in range(KA - 1):
        ya = ya + dwa_ref[k:k + 1, :] * sa_ref[k]
    yb = dwb_ref[KB - 1:KB, :] * ub_ref[...]
    for k in range(KB - 1):
        yb = yb + dwb_ref[k:k + 1, :] * sb_ref[k]
    ya = _ln_silu(ya + ba_ref[...], lng_ref[...], lnb_ref[...])
    yb = gb_ref[...] * yb
    yc = jnp.concatenate([ya.astype(BF16), yb.astype(BF16)], axis=1)
    o_ref[...] = x_ref[...] + _dot(yc, wo_ref[...])


def _conv_sample(sa_t, ua, sb_t, ub, gb, x, dwa, ba, lng, lnb, dwb, w_out):
    n, d = x.shape
    args = (sa_t, ua, sb_t, ub, gb, x, dwa, ba, lng, lnb, dwb, w_out)
    return pl.pallas_call(
        _conv_s_body,
        grid=(1,),
        in_specs=[_resident(a.shape) for a in args],
        out_specs=pl.BlockSpec((n, d), lambda i: (0, 0)),
        out_shape=jax.ShapeDtypeStruct((n, d), F32),
        compiler_params=_params(("arbitrary",)),
        name="conv_sample",
    )(*args)


def _mla_proj_body(x_ref, g_ref, cos_ref, sin_ref, wdq_ref, qn_ref, wqn_ref, wqr_ref, wqp_ref, gqn_ref, gqr_ref,
                   gqp_ref, gkn_ref, wkc_ref, kvn_ref, wkr_ref, wkp_ref, gkr_ref, gkp_ref, wuk_ref, wukt_ref,
                   sel_ref, selt_ref, ones_ref,
                   ckv_ref, kr_ref, ks_ref, ckvb_ref, krb_ref, ql_ref, qr_ref):
    nh = ql_ref.shape[0]
    h = _rms(x_ref[...], g_ref[...]).astype(BF16)
    cos = cos_ref[...]
    sin = sin_ref[...]
    sel = sel_ref[...]

    cq = _rms(_dot(h, wdq_ref[...]), qn_ref[...]).astype(BF16)
    qn = _dot(cq, wqn_ref[...])
    qr = _dot(cq, wqr_ref[...])
    qp = _dot(cq, wqp_ref[...])
    ssq = _hl_dot(qn * qn + qr * qr, sel)
    rq = _hl_dot(lax.rsqrt(ssq * (1.0 / QK_DIM) + EPS), selt_ref[...])
    cos_h = jnp.tile(cos, (1, nh))
    sin_h = jnp.tile(sin, (1, nh))
    qrot = (qr * rq * gqr_ref[...]) * cos_h + (qp * rq * gqp_ref[...]) * sin_h
    qnn = (qn * rq * gqn_ref[...]) * gkn_ref[...]
    for i in range(nh):
        sl = slice(i * LANES, (i + 1) * LANES)
        ql_ref[i] = _dot(qnn[:, sl].astype(BF16), wukt_ref[i]).astype(BF16)
        qr_ref[i] = qrot[:, sl].astype(BF16)

    ckv = _rms(_dot(h, wkc_ref[...]), kvn_ref[...])
    ckv_ref[...] = ckv
    ckvb = ckv.astype(BF16)
    ckvb_ref[...] = ckvb
    kraw = _dot(h, wkr_ref[...])
    kpar = _dot(h, wkp_ref[...])
    kn = _dot(ckvb, wuk_ref[...])
    ssk = _hl_dot(kn * kn, sel) + _hl_dot(kraw * kraw, ones_ref[...])
    ks_ref[...] = lax.rsqrt(ssk * (1.0 / QK_DIM) + EPS)
    kr = (kraw * gkr_ref[...]) * cos + (kpar * gkp_ref[...]) * sin
    kr_ref[...] = kr[:, 0:QK_ROPE]
    krb_ref[...] = kr.astype(BF16)


def _mla_proj(x, gain, cos_t, sin_t, weights, bm):
    n, d = x.shape
    nh = N_HEADS
    n_tab = cos_t.shape[0] // bm
    outs = [jax.ShapeDtypeStruct((n, KV_LORA), F32), jax.ShapeDtypeStruct((n, QK_ROPE), F32),
            jax.ShapeDtypeStruct((n, nh), F32), jax.ShapeDtypeStruct((n, KV_LORA), BF16),
            jax.ShapeDtypeStruct((n, LANES), BF16), jax.ShapeDtypeStruct((nh, n, KV_LORA), BF16),
            jax.ShapeDtypeStruct((nh, n, LANES), BF16)]

    def row(w):
        return pl.BlockSpec((bm, w), lambda m: (m, 0))

    tab = pl.BlockSpec((bm, LANES), lambda m: (m % n_tab, 0))
    return pl.pallas_call(
        _mla_proj_body,
        grid=(n // bm,),
        in_specs=[row(d), _resident(gain.shape), tab, tab] + [_resident(w.shape) for w in weights],
        out_specs=[row(KV_LORA), row(QK_ROPE), row(nh), row(KV_LORA), row(LANES),
                   pl.BlockSpec((nh, bm, KV_LORA), lambda m: (0, m, 0)),
                   pl.BlockSpec((nh, bm, LANES), lambda m: (0, m, 0))],
        out_shape=outs,
        compiler_params=_params(("parallel",)),
        name="mla_proj",
    )(x, gain, cos_t, sin_t, *weights)


HEADS_PER_STEP = 4


def _attn_p_body(ql_ref, qr_ref, kc_ref, kr_ref, kst_ref, o_ref, m_ref, l_ref, acc_ref, s_ref, *, bq, bk):
    hg = pl.program_id(1)
    qi = pl.program_id(2)
    ki = pl.program_id(3)
    nhs = ql_ref.shape[0]

    @pl.when(ki == 0)
    def _():
        m_ref[...] = jnp.full(m_ref.shape, NEG, F32)
        l_ref[...] = jnp.zeros(l_ref.shape, F32)
        acc_ref[...] = jnp.zeros(acc_ref.shape, F32)

    def step(keep):
        kc = kc_ref[...]
        kr = kr_ref[...]
        for i in range(nhs):
            s = _dot_nt(ql_ref[i], kc) + _dot_nt(qr_ref[i], kr)
            s = s * (kst_ref[pl.ds(hg * nhs + i, 1), :] * (ATTN_SCALE * LOG2E))
            s_ref[i * bq:(i + 1) * bq, :] = s if keep is None else jnp.where(keep, s, NEG)
        s = s_ref[...]
        m_prev = m_ref[...]
        m_new = jnp.maximum(m_prev, jnp.max(s, axis=-1, keepdims=True))
        alpha = jnp.exp2(m_prev - m_new)
        p = jnp.exp2(s - m_new)
        l_ref[...] = alpha * l_ref[...] + jnp.sum(p, axis=-1, keepdims=True)
        acc_ref[...] = alpha * acc_ref[...] + _dot(p.astype(BF16), kc)
        m_ref[...] = m_new

    @pl.when(ki < qi)
    def _():
        step(None)

    @pl.when(ki == qi)
    def _():
        step(lax.broadcasted_iota(jnp.int32, (bq, bk), 1) <= lax.broadcasted_iota(jnp.int32, (bq, bk), 0))
        o = acc_ref[...] / l_ref[...]
        for i in range(nhs):
            o_ref[i] = o[i * bq:(i + 1) * bq, :].astype(BF16)


def _attn_prompt(ql, qr, kc, kr, kst, bq):
    nh, b, t, c = ql.shape
    bk = bq
    nhs = HEADS_PER_STEP
    body = functools.partial(_attn_p_body, bq=bq, bk=bk)
    return pl.pallas_call(
        body,
        grid=(b, nh // nhs, t // bq, t // bk),
        in_specs=[pl.BlockSpec((nhs, None, bq, c), lambda i, h, q, k: (h, i, q, 0)),
                  pl.BlockSpec((nhs, None, bq, LANES), lambda i, h, q, k: (h, i, q, 0)),
                  pl.BlockSpec((None, bk, c), lambda i, h, q, k: (i, jnp.minimum(k, q), 0)),
                  pl.BlockSpec((None, bk, LANES), lambda i, h, q, k: (i, jnp.minimum(k, q), 0)),
                  pl.BlockSpec((None, nh, bk), lambda i, h, q, k: (i, 0, jnp.minimum(k, q)))],
        out_specs=pl.BlockSpec((nhs, None, bq, c), lambda i, h, q, k: (h, i, q, 0)),
        out_shape=jax.ShapeDtypeStruct((nh, b, t, c), BF16),
        scratch_shapes=[pltpu.VMEM((nhs * bq, 1), F32), pltpu.VMEM((nhs * bq, 1), F32),
                        pltpu.VMEM((nhs * bq, c), F32), pltpu.VMEM((nhs * bq, bk), F32)],
        compiler_params=_params(("parallel", "parallel", "parallel", "arbitrary")),
        name="attn_prompt",
    )(ql, qr, kc, kr, kst)


def _attn_s_body(pt_ref, ql_ref, qr_ref, kcs_ref, krs_ref, kss_ref, cc_hbm, crt_hbm, cst_hbm, o_ref,
                 cbuf, rbuf, sbuf, kcb, sem, *, n_chunks):
    nh = ql_ref.shape[0]
    b = pl.program_id(0)
    total = pl.num_programs(0) * n_chunks

    def copies(g, slot):
        seq = g // n_chunks
        base = (g % n_chunks) * PAGES_PER_CHUNK
        out = []
        for i in range(PAGES_PER_CHUNK):
            page = pt_ref[seq, base + i]
            keys = pl.ds(i * PAGE_SIZE, PAGE_SIZE)
            out.append(pltpu.make_async_copy(cc_hbm.at[page], cbuf.at[slot, keys], sem.at[0, slot]))
            out.append(pltpu.make_async_copy(crt_hbm.at[page], rbuf.at[slot, :, keys], sem.at[1, slot]))
            out.append(pltpu.make_async_copy(cst_hbm.at[page], sbuf.at[slot, :, keys], sem.at[2, slot]))
        return out

    @pl.when(b == 0)
    def _():
        for g0 in range(CHUNK_SLOTS - 1):
            for cp in copies(g0, g0):
                cp.start()

    ql = ql_ref[...]
    qr = qr_ref[...]
    qr_rope = qr[:, 0:QK_ROPE]

    kc_self = kcs_ref[...]
    s_self = (jnp.sum(ql.astype(F32) * kc_self.astype(F32), axis=-1, keepdims=True)
              + jnp.sum(qr.astype(F32) * krs_ref[...].astype(F32), axis=-1, keepdims=True))
    m0 = s_self * (kss_ref[...] * (ATTN_SCALE * LOG2E))
    l0 = jnp.ones((nh, 1), F32)
    acc0 = jnp.broadcast_to(kc_self.astype(F32), (nh, KV_LORA))

    def step(c, carry):
        m_prev, l_prev, acc = carry
        g = b * n_chunks + c
        slot = g % CHUNK_SLOTS
        ahead = g + (CHUNK_SLOTS - 1)

        @pl.when(ahead < total)
        def _():
            for cp in copies(ahead, ahead % CHUNK_SLOTS):
                cp.start()

        for cp in copies(g, slot):
            cp.wait()

        kcb[...] = cbuf[slot].astype(BF16)
        kc = kcb[...]
        s = (_dot_nt(ql, kc) + _dot(qr_rope, rbuf[slot].astype(BF16))) * (sbuf[slot] * (ATTN_SCALE * LOG2E))
        m_new = jnp.maximum(m_prev, jnp.max(s, axis=-1, keepdims=True))
        alpha = jnp.exp2(m_prev - m_new)
        p = jnp.exp2(s - m_new)
        l_new = alpha * l_prev + jnp.sum(p, axis=-1, keepdims=True)
        acc_new = alpha * acc + _dot(p.astype(BF16), kc)
        return m_new, l_new, acc_new

    _, l_fin, acc_fin = lax.fori_loop(0, n_chunks, step, (m0, l0, acc0))
    o_ref[...] = (acc_fin / l_fin).astype(BF16)


def _attn_sample(page_table, ql, qr, kc_self, kr_self, ks_self, cache_c, cache_r, cache_s):
    n, nh, c = ql.shape
    n_chunks = page_table.shape[1] // PAGES_PER_CHUNK
    assert n * n_chunks >= CHUNK_SLOTS - 1
    rows = PAGES_PER_CHUNK * PAGE_SIZE
    body = functools.partial(_attn_s_body, n_chunks=n_chunks)
    any_spec = pl.BlockSpec(memory_space=pl.ANY)
    grid_spec = pltpu.PrefetchScalarGridSpec(
        num_scalar_prefetch=1,
        grid=(n,),
        in_specs=[pl.BlockSpec((None, nh, c), lambda i, pt: (i, 0, 0)),
                  pl.BlockSpec((None, nh, LANES), lambda i, pt: (i, 0, 0)),
                  pl.BlockSpec((None, 1, c), lambda i, pt: (i, 0, 0)),
                  pl.BlockSpec((None, 1, LANES), lambda i, pt: (i, 0, 0)),
                  pl.BlockSpec((None, nh, 1), lambda i, pt: (i, 0, 0)),
                  any_spec, any_spec, any_spec],
        out_specs=pl.BlockSpec((None, nh, c), lambda i, pt: (i, 0, 0)),
        scratch_shapes=[pltpu.VMEM((CHUNK_SLOTS, rows, c), F32), pltpu.VMEM((CHUNK_SLOTS, QK_ROPE, rows), F32),
                        pltpu.VMEM((CHUNK_SLOTS, nh, rows), F32), pltpu.VMEM((rows, c), BF16),
                        pltpu.SemaphoreType.DMA((3, CHUNK_SLOTS))],
    )
    return pl.pallas_call(
        body,
        grid_spec=grid_spec,
        out_shape=jax.ShapeDtypeStruct((n, nh, c), BF16),
        compiler_params=_params(("arbitrary",)),
        name="attn_sample",
    )(page_table, ql, qr, kc_self, kr_self, ks_self, cache_c, cache_r, cache_s)


def _mla_out_body(ol_ref, x_ref, wuv_ref, wo_ref, o_ref, ov_ref):
    nh = ol_ref.shape[0]
    for i in range(nh):
        ov_ref[:, i * LANES:(i + 1) * LANES] = _dot(ol_ref[i], wuv_ref[i]).astype(BF16)
    o_ref[...] = x_ref[...] + _dot(ov_ref[...], wo_ref[...])


def _mla_out(o_lat, x, w_uv_h, w_o, bm):
    nh, n, c = o_lat.shape
    d = x.shape[1]
    return pl.pallas_call(
        _mla_out_body,
        grid=(n // bm,),
        in_specs=[pl.BlockSpec((nh, bm, c), lambda m: (0, m, 0)),
                  pl.BlockSpec((bm, d), lambda m: (m, 0)),
                  _resident(w_uv_h.shape), _resident(w_o.shape)],
        out_specs=pl.BlockSpec((bm, d), lambda m: (m, 0)),
        out_shape=jax.ShapeDtypeStruct((n, d), F32),
        scratch_shapes=[pltpu.VMEM((bm, nh * LANES), BF16)],
        compiler_params=_params(("parallel",)),
        name="mla_out",
    )(o_lat, x, w_uv_h, w_o)


def _rope_tables(pos):
    half = QK_ROPE // 2
    inv_freq = ROPE_THETA ** (-jnp.arange(half, dtype=F32) / half)
    ang = pos.astype(F32)[:, None] * inv_freq[None, :]
    cos, sin = jnp.cos(ang), jnp.sin(ang)
    z = jnp.zeros((pos.shape[0], LANES - QK_ROPE), F32)
    return jnp.concatenate([cos, cos, z], axis=1), jnp.concatenate([-sin, sin, z], axis=1)


def _swap_halves(a):
    half = QK_ROPE // 2
    return jnp.concatenate([a[..., half:], a[..., :half]], axis=-1)


def _pad_rope(a):
    return jnp.concatenate([a, jnp.zeros(a.shape[:-1] + (LANES - QK_ROPE,), a.dtype)], axis=-1)


def _mla_weights(w_dq, q_norm, w_uq, w_dkv, kv_norm, w_uk, gain_q, gain_k):
    nh = N_HEADS
    lq = w_uq.shape[0]
    w_rope = w_uq[:, :, QK_NOPE:]
    per_head = lambda a: a.reshape(lq, nh * LANES).astype(BF16)
    tile = lambda v: jnp.tile(v, nh)[None, :]
    head_of_lane = jnp.arange(nh * LANES) // LANES
    sel = (head_of_lane[:, None] == jnp.arange(nh)[None, :]).astype(BF16)
    w_kr = w_dkv[:, KV_LORA:]
    return [
        w_dq.astype(BF16), q_norm[None, :],
        per_head(w_uq[:, :, :QK_NOPE]), per_head(_pad_rope(w_rope)), per_head(_pad_rope(_swap_halves(w_rope))),
        tile(gain_q[:QK_NOPE]), tile(_pad_rope(gain_q[QK_NOPE:])), tile(_pad_rope(_swap_halves(gain_q[QK_NOPE:]))),
        tile(gain_k[:QK_NOPE]),
        w_dkv[:, :KV_LORA].astype(BF16), kv_norm[None, :],
        _pad_rope(w_kr).astype(BF16), _pad_rope(_swap_halves(w_kr)).astype(BF16),
        _pad_rope(gain_k[QK_NOPE:])[None, :], _pad_rope(_swap_halves(gain_k[QK_NOPE:]))[None, :],
        w_uk.reshape(w_uk.shape[0], nh * QK_NOPE).astype(BF16),
        jnp.transpose(w_uk, (1, 2, 0)).astype(BF16),
        sel, jnp.transpose(sel), jnp.ones((LANES, nh), BF16),
    ]


def _tile_rows(n, pref):
    return pref if n % pref == 0 else n


def kernel(x_prompt, x_sample, p_prompt, p_sample, state_conv_a, state_conv_b, cache_ckv, cache_krope,
           cache_kscale, page_table, norm_gains, ffn_w_gu, ffn_w_down, ple_w_gate, ple_w_proj, conv_w_in,
           conv_dw_a, conv_dw_a_bias, conv_ln_gain, conv_ln_bias, conv_dw_b, conv_w_out, mla_w_dq, mla_q_norm,
           mla_w_uq, mla_w_dkv, mla_kv_norm, mla_w_uk, mla_w_uv, mla_qk_gain_q, mla_qk_gain_k, mla_w_o):
    bsz, seq, d = x_prompt.shape
    dec = x_sample.shape[0]
    depth = norm_gains.shape[0]
    nh = N_HEADS
    past_len = page_table.shape[1] * PAGE_SIZE
    n_p = bsz * seq

    xp = x_prompt.reshape(n_p, d)
    xs = x_sample.reshape(dec, d)
    bm_p = _tile_rows(n_p, 512)
    bm_ffn = _tile_rows(n_p, 1024)
    bf = _tile_rows(ffn_w_down.shape[2], 512)
    w_gu = ffn_w_gu.astype(BF16)
    w_down = ffn_w_down.astype(BF16)
    w_gate = ple_w_gate.astype(BF16)
    w_proj = ple_w_proj.astype(BF16)
    pp = p_prompt.reshape(depth, n_p, -1)
    ps = p_sample.reshape(depth, dec, -1)

    cos_p, sin_p = _rope_tables(jnp.arange(seq, dtype=jnp.int32))
    cos_s, sin_s = _rope_tables(jnp.full((dec,), past_len, jnp.int32))

    ca_p, ca_s, cb_p, cb_s = [], [], [], []
    ckv_p, ckv_s, kr_p, kr_s, ks_p, ks_s = [], [], [], [], [], []
    for i in range(depth):
        j = i // 2
        gains = norm_gains[i][:, None, :]

        xp = _ffn(xp, gains[0], w_gu, w_down, i, 0, bm_ffn, bf)
        xs = _ffn(xs, gains[0], w_gu, w_down, i, 0, dec, bf)

        if i % 2 == 0:
            w_in = conv_w_in[j].astype(BF16)
            w_out = conv_w_out[j].astype(BF16)
            dwa = jnp.concatenate([conv_dw_a[j], jnp.zeros((1, CONV_WIDTH), F32)], axis=0)
            dwb = jnp.concatenate([conv_dw_b[j], jnp.zeros((8 - KB, CONV_WIDTH), F32)], axis=0)
            ba, lng, lnb = conv_dw_a_bias[j][None, :], conv_ln_gain[j][None, :], conv_ln_bias[j][None, :]

            ua, ub, gb = _conv_in(xp, gains[1], w_in, bm_p, 512)
            ua3, ub3 = ua.reshape(bsz, seq, CONV_WIDTH), ub.reshape(bsz, seq, CONV_WIDTH)
            xp = _conv_prompt(ua3, ub3, gb.reshape(bsz, seq, CONV_WIDTH), xp.reshape(bsz, seq, d),
                              dwa, ba, lng, lnb, dwb, w_out, _tile_rows(seq, 512)).reshape(n_p, d)
            ca_p.append(ua3[:, seq - (KA - 1):])
            cb_p.append(ub3[:, seq - (KB - 1):])

            ua, ub, gb = _conv_in(xs, gains[1], w_in, dec, 512)
            sa, sb = state_conv_a[j], state_conv_b[j]
            xs = _conv_sample(jnp.swapaxes(sa, 0, 1), ua, jnp.swapaxes(sb, 0, 1), ub, gb, xs,
                              dwa, ba, lng, lnb, dwb, w_out)
            ca_s.append(jnp.concatenate([sa[:, 1:], ua[:, None, :]], axis=1))
            cb_s.append(jnp.concatenate([sb[:, 1:], ub[:, None, :]], axis=1))
        else:
            weights = _mla_weights(mla_w_dq[j], mla_q_norm[j], mla_w_uq[j], mla_w_dkv[j], mla_kv_norm[j],
                                   mla_w_uk[j], mla_qk_gain_q[j], mla_qk_gain_k[j])
            w_uv_h = jnp.transpose(mla_w_uv[j], (1, 0, 2)).astype(BF16)
            w_o = mla_w_o[j].astype(BF16)

            bmq = _tile_rows(seq, 256)
            ckv, kr, ks, ckvb, krb, ql, qr = _mla_proj(xp, gains[1], cos_p, sin_p, weights, bmq)
            kst = jnp.swapaxes(ks.reshape(bsz, seq, nh), 1, 2)
            o_lat = _attn_prompt(ql.reshape(nh, bsz, seq, KV_LORA), qr.reshape(nh, bsz, seq, LANES),
                                 ckvb.reshape(bsz, seq, KV_LORA), krb.reshape(bsz, seq, LANES), kst,
                                 _tile_rows(seq, 512))
            xp = _mla_out(o_lat.reshape(nh, n_p, KV_LORA), xp, w_uv_h, w_o, bm_p)
            npg = seq // PAGE_SIZE
            ckv_p.append(ckv.reshape(bsz, npg, PAGE_SIZE, KV_LORA))
            kr_p.append(kr.reshape(bsz, npg, PAGE_SIZE, QK_ROPE))
            ks_p.append(ks.reshape(bsz, npg, PAGE_SIZE, nh))

            ckv, kr, ks, ckvb, krb, ql, qr = _mla_proj(xs, gains[1], cos_s, sin_s, weights, dec)
            o_lat = _attn_sample(page_table, jnp.swapaxes(ql, 0, 1), jnp.swapaxes(qr, 0, 1), ckvb[:, None, :],
                                 krb[:, None, :], ks[:, :, None], cache_ckv[j],
                                 jnp.swapaxes(cache_krope[j], 1, 2), jnp.swapaxes(cache_kscale[j], 1, 2))
            xs = _mla_out(jnp.swapaxes(o_lat, 0, 1), xs, w_uv_h, w_o, dec)
            ckv_s.append(ckv[:, None, :])
            kr_s.append(kr[:, None, :])
            ks_s.append(ks[:, None, :])

        xp = _ffn(xp, gains[2], w_gu, w_down, i, 1, bm_ffn, bf)
        xs = _ffn(xs, gains[2], w_gu, w_down, i, 1, dec, bf)
        xp = _ple(xp, gains[3], w_gate, pp, w_proj, i, bm_p)
        xs = _ple(xs, gains[3], w_gate, ps, w_proj, i, dec)

    return (xp.reshape(bsz, seq, d), xs.reshape(dec, 1, d),
            jnp.stack(ca_p), jnp.stack(ca_s), jnp.stack(cb_p), jnp.stack(cb_s),
            jnp.stack(ckv_p), jnp.stack(ckv_s), jnp.stack(kr_p), jnp.stack(kr_s),
            jnp.stack(ks_p), jnp.stack(ks_s))
```

```python
import functools

import jax
import jax.numpy as jnp
from jax import lax
from jax.experimental import pallas as pl
from jax.experimental.pallas import tpu as pltpu

F32 = jnp.float32
BF16 = jnp.bfloat16

EPS = 1e-6
ROPE_THETA = 10000.0
NEG = -1e30
PAGE_SIZE = 128
N_HEADS = 16
QK_NOPE = 128
QK_ROPE = 64
QK_DIM = QK_NOPE + QK_ROPE
KV_LORA = 512
KA = 31
KB = 3
CONV_WIDTH = 1024
ATTN_SCALE = QK_DIM ** -0.5
LOG2E = 1.4426950408889634
LANES = 128
SUBLANES = 8
VMEM_LIMIT = 56 * 1024 * 1024
PAGES_PER_CHUNK = 32
CHUNK_SLOTS = 3


def _params(semantics):
    return pltpu.CompilerParams(dimension_semantics=semantics, vmem_limit_bytes=VMEM_LIMIT)


def _resident(shape):
    nd = len(shape)
    return pl.BlockSpec(shape, lambda *_: (0,) * nd, pipeline_mode=pl.Buffered(1))


def _rms(x, g):
    return x * lax.rsqrt(jnp.mean(x * x, axis=-1, keepdims=True) + EPS) * g


def _dot(a, b):
    return jnp.dot(a, b, preferred_element_type=F32)


def _dot_nt(a, b):
    return lax.dot_general(a, b, (((1,), (1,)), ((), ())), preferred_element_type=F32)


def _hl_dot(v, sel):
    hi = v.astype(BF16)
    lo = (v - hi.astype(F32)).astype(BF16)
    return _dot(hi, sel) + _dot(lo, sel)


def _ffn_body(x_ref, g_ref, wg_ref, wu_ref, wd_ref, o_ref, h_ref):
    @pl.when(pl.program_id(1) == 0)
    def _():
        x = x_ref[...]
        h_ref[...] = _rms(x, g_ref[...]).astype(BF16)
        o_ref[...] = x

    h = h_ref[...]
    g = _dot(h, wg_ref[...])
    u = _dot(h, wu_ref[...])
    a = (0.5 * g * jax.nn.sigmoid(g) * u).astype(BF16)
    o_ref[...] += _dot(a, wd_ref[...])


def _ffn(x, gain, w_gu, w_down, layer, which, bm, bf):
    n, d = x.shape
    nf = w_down.shape[2] // bf
    return pl.pallas_call(
        _ffn_body,
        grid=(n // bm, nf),
        in_specs=[pl.BlockSpec((bm, d), lambda m, f: (m, 0)),
                  pl.BlockSpec((1, d), lambda m, f: (0, 0)),
                  pl.BlockSpec((None, None, d, bf), lambda m, f: (layer, which, 0, f)),
                  pl.BlockSpec((None, None, d, bf), lambda m, f: (layer, which, 0, nf + f)),
                  pl.BlockSpec((None, None, bf, d), lambda m, f: (layer, which, f, 0))],
        out_specs=pl.BlockSpec((bm, d), lambda m, f: (m, 0)),
        out_shape=jax.ShapeDtypeStruct((n, d), F32),
        scratch_shapes=[pltpu.VMEM((bm, d), BF16)],
        compiler_params=_params(("parallel", "arbitrary")),
        name="ffn",
    )(x, gain, w_gu, w_gu, w_down)


def _layer_resident(shape, layer):
    nd = len(shape) - 1
    return pl.BlockSpec((None,) + tuple(shape[1:]), lambda *_: (layer,) + (0,) * nd, pipeline_mode=pl.Buffered(1))


def _ple_body(x_ref, g_ref, wg_ref, p_ref, wp_ref, o_ref):
    x = x_ref[...]
    h = _rms(x, g_ref[...]).astype(BF16)
    gate = jax.nn.sigmoid(_dot(h, wg_ref[...]))
    o_ref[...] = x + gate * _dot(p_ref[...].astype(BF16), wp_ref[...])


def _ple(x, gain, w_gate, p, w_proj, layer, bm):
    n, d = x.shape
    pd = p.shape[2]
    return pl.pallas_call(
        _ple_body,
        grid=(n // bm,),
        in_specs=[pl.BlockSpec((bm, d), lambda m: (m, 0)),
                  _resident((1, d)),
                  _layer_resident(w_gate.shape, layer),
                  pl.BlockSpec((None, bm, pd), lambda m: (layer, m, 0)),
                  _layer_resident(w_proj.shape, layer)],
        out_specs=pl.BlockSpec((bm, d), lambda m: (m, 0)),
        out_shape=jax.ShapeDtypeStruct((n, d), F32),
        compiler_params=_params(("parallel",)),
        name="ple",
    )(x, gain, w_gate, p, w_proj)


def _conv_in_body(x_ref, g_ref, wav_ref, wag_ref, wbh_ref, wbb_ref, wbc_ref, ua_ref, ub_ref, gb_ref, h_ref):
    @pl.when(pl.program_id(1) == 0)
    def _():
        h_ref[...] = _rms(x_ref[...], g_ref[...]).astype(BF16)

    h = h_ref[...]
    ua_ref[...] = _dot(h, wav_ref[...]) * jax.nn.sigmoid(_dot(h, wag_ref[...]))
    ub_ref[...] = _dot(h, wbc_ref[...]) * _dot(h, wbh_ref[...])
    gb_ref[...] = _dot(h, wbb_ref[...])


def _conv_in(x, gain, w_in, bm, bc):
    n, d = x.shape
    c = CONV_WIDTH
    nc = c // bc

    def wspec(k):
        return pl.BlockSpec((d, bc), lambda m, j: (0, k * nc + j))

    out = jax.ShapeDtypeStruct((n, c), F32)
    ospec = pl.BlockSpec((bm, bc), lambda m, j: (m, j))
    return pl.pallas_call(
        _conv_in_body,
        grid=(n // bm, nc),
        in_specs=[pl.BlockSpec((bm, d), lambda m, j: (m, 0)),
                  pl.BlockSpec((1, d), lambda m, j: (0, 0)),
                  wspec(0), wspec(1), wspec(2), wspec(3), wspec(4)],
        out_specs=[ospec, ospec, ospec],
        out_shape=[out, out, out],
        scratch_shapes=[pltpu.VMEM((bm, d), BF16)],
        compiler_params=_params(("parallel", "arbitrary")),
        name="conv_in",
    )(x, gain, w_in, w_in, w_in, w_in, w_in)


def _ln_silu(y, g, b):
    yc = y - jnp.mean(y, axis=-1, keepdims=True)
    z = yc * lax.rsqrt(jnp.mean(yc * yc, axis=-1, keepdims=True) + EPS) * g + b
    return z * jax.nn.sigmoid(z)


HALO_A = 32
HALO_B = 8
CONV_ROWS = 64


def _conv_p_body(ua_ref, uah_ref, ub_ref, ubh_ref, gb_ref, x_ref, dwa_ref, ba_ref, lng_ref, lnb_ref, dwb_ref,
                 wo_ref, o_ref, pa_ref, pb_ref, ya_ref, yc_ref, sh_ref, *, bt):
    first = pl.program_id(1) == 0
    pa_ref[0:HALO_A, :] = jnp.where(first, 0.0, uah_ref[...])
    pa_ref[HALO_A:HALO_A + bt, :] = ua_ref[...]
    pb_ref[0:HALO_B, :] = jnp.where(first, 0.0, ubh_ref[...])
    pb_ref[HALO_B:HALO_B + bt, :] = ub_ref[...]
    c = pa_ref.shape[1]
    sh_rows = sh_ref.shape[1]

    def chunk(cc, carry):
        cs = pl.multiple_of(cc * LANES, LANES)
        wa = dwa_ref[:, pl.ds(cs, LANES)]
        wb = dwb_ref[:, pl.ds(cs, LANES)]
        for s in range(1, SUBLANES):
            sh_ref[s - 1] = pa_ref[pl.ds(s, sh_rows), pl.ds(cs, LANES)]
        for r in range(bt // CONV_ROWS):
            acc = jnp.zeros((CONV_ROWS, LANES), F32)
            for k in range(KA):
                off = HALO_A - (KA - 1) + k
                s = off % SUBLANES
                base = r * CONV_ROWS + off - s
                if s == 0:
                    tap = pa_ref[pl.ds(base, CONV_ROWS), pl.ds(cs, LANES)]
                else:
                    tap = sh_ref[s - 1, pl.ds(base, CONV_ROWS), :]
                acc = acc + wa[k:k + 1, :] * tap
            ya_ref[pl.ds(r * CONV_ROWS, CONV_ROWS), pl.ds(cs, LANES)] = acc
            accb = jnp.zeros((CONV_ROWS, LANES), F32)
            for k in range(KB):
                accb = accb + wb[k:k + 1, :] * pb_ref[pl.ds(r * CONV_ROWS + HALO_B - (KB - 1) + k, CONV_ROWS),
                                                      pl.ds(cs, LANES)]
            yb = gb_ref[pl.ds(r * CONV_ROWS, CONV_ROWS), pl.ds(cs, LANES)] * accb
            yc_ref[pl.ds(r * CONV_ROWS, CONV_ROWS), pl.ds(c + cs, LANES)] = yb.astype(BF16)
        return carry

    lax.fori_loop(0, c // LANES, chunk, 0)
    ya = _ln_silu(ya_ref[...] + ba_ref[...], lng_ref[...], lnb_ref[...])
    yc_ref[:, 0:c] = ya.astype(BF16)
    o_ref[...] = x_ref[...] + _dot(yc_ref[...], wo_ref[...])


def _conv_prompt(ua, ub, gb, x, dwa, ba, lng, lnb, dwb, w_out, bt):
    b, t, c = ua.shape
    d = x.shape[2]
    body = functools.partial(_conv_p_body, bt=bt)
    cur = pl.BlockSpec((None, bt, c), lambda i, j: (i, j, 0))
    return pl.pallas_call(
        body,
        grid=(b, t // bt),
        in_specs=[cur,
                  pl.BlockSpec((None, HALO_A, c), lambda i, j: (i, jnp.maximum(j * (bt // HALO_A) - 1, 0), 0)),
                  cur,
                  pl.BlockSpec((None, HALO_B, c), lambda i, j: (i, jnp.maximum(j * (bt // HALO_B) - 1, 0), 0)),
                  cur,
                  pl.BlockSpec((None, bt, d), lambda i, j: (i, j, 0)),
                  _resident(dwa.shape), _resident(ba.shape), _resident(lng.shape), _resident(lnb.shape),
                  _resident(dwb.shape), _resident(w_out.shape)],
        out_specs=pl.BlockSpec((None, bt, d), lambda i, j: (i, j, 0)),
        out_shape=jax.ShapeDtypeStruct((b, t, d), F32),
        scratch_shapes=[pltpu.VMEM((HALO_A + bt, c), F32), pltpu.VMEM((HALO_B + bt, c), F32),
                        pltpu.VMEM((bt, c), F32), pltpu.VMEM((bt, 2 * c), BF16),
                        pltpu.VMEM((SUBLANES - 1, bt + HALO_A - SUBLANES, LANES), F32)],
        compiler_params=_params(("parallel", "arbitrary")),
        name="conv_prompt",
    )(ua, ua, ub, ub, gb, x, dwa, ba, lng, lnb, dwb, w_out)


def _conv_s_body(sa_ref, ua_ref, sb_ref, ub_ref, gb_ref, x_ref, dwa_ref, ba_ref, lng_ref, lnb_ref, dwb_ref,
                 wo_ref, o_ref):
    ya = dwa_ref[KA - 1:KA, :] * ua_ref[...]
    for k in range(KA - 1):
        ya = ya + dwa_ref[k:k + 1, :] * sa_ref[k]
    yb = dwb_ref[KB - 1:KB, :] * ub_ref[...]
    for k in range(KB - 1):
        yb = yb + dwb_ref[k:k + 1, :] * sb_ref[k]
    ya = _ln_silu(ya + ba_ref[...], lng_ref[...], lnb_ref[...])
    yb = gb_ref[...] * yb
    yc = jnp.concatenate([ya.astype(BF16), yb.astype(BF16)], axis=1)
    o_ref[...] = x_ref[...] + _dot(yc, wo_ref[...])


def _conv_sample(sa_t, ua, sb_t, ub, gb, x, dwa, ba, lng, lnb, dwb, w_out):
    n, d = x.shape
    args = (sa_t, ua, sb_t, ub, gb, x, dwa, ba, lng, lnb, dwb, w_out)
    return pl.pallas_call(
        _conv_s_body,
        grid=(1,),
        in_specs=[_resident(a.shape) for a in args],
        out_specs=pl.BlockSpec((n, d), lambda i: (0, 0)),
        out_shape=jax.ShapeDtypeStruct((n, d), F32),
        compiler_params=_params(("arbitrary",)),
        name="conv_sample",
    )(*args)


def _mla_proj_body(x_ref, g_ref, cos_ref, sin_ref, wdq_ref, qn_ref, wqn_ref, wqr_ref, wqp_ref, gqn_ref, gqr_ref,
                   gqp_ref, gkn_ref, wkc_ref, kvn_ref, wkr_ref, wkp_ref, gkr_ref, gkp_ref, wuk_ref, wukt_ref,
                   sel_ref, selt_ref, ones_ref,
                   ckv_ref, kr_ref, ks_ref, ckvb_ref, krb_ref, ql_ref, qr_ref):
    nh = ql_ref.shape[0]
    h = _rms(x_ref[...], g_ref[...]).astype(BF16)
    cos = cos_ref[...]
    sin = sin_ref[...]
    sel = sel_ref[...]

    cq = _rms(_dot(h, wdq_ref[...]), qn_ref[...]).astype(BF16)
    qn = _dot(cq, wqn_ref[...])
    qr = _dot(cq, wqr_ref[...])
    qp = _dot(cq, wqp_ref[...])
    ssq = _hl_dot(qn * qn + qr * qr, sel)
    rq = _hl_dot(lax.rsqrt(ssq * (1.0 / QK_DIM) + EPS), selt_ref[...])
    cos_h = jnp.tile(cos, (1, nh))
    sin_h = jnp.tile(sin, (1, nh))
    qrot = (qr * rq * gqr_ref[...]) * cos_h + (qp * rq * gqp_ref[...]) * sin_h
    qnn = (qn * rq * gqn_ref[...]) * gkn_ref[...]
    for i in range(nh):
        sl = slice(i * LANES, (i + 1) * LANES)
        ql_ref[i] = _dot(qnn[:, sl].astype(BF16), wukt_ref[i]).astype(BF16)
        qr_ref[i] = qrot[:, sl].astype(BF16)

    ckv = _rms(_dot(h, wkc_ref[...]), kvn_ref[...])
    ckv_ref[...] = ckv
    ckvb = ckv.astype(BF16)
    ckvb_ref[...] = ckvb
    kraw = _dot(h, wkr_ref[...])
    kpar = _dot(h, wkp_ref[...])
    kn = _dot(ckvb, wuk_ref[...])
    ssk = _hl_dot(kn * kn, sel) + _hl_dot(kraw * kraw, ones_ref[...])
    ks_ref[...] = lax.rsqrt(ssk * (1.0 / QK_DIM) + EPS)
    kr = (kraw * gkr_ref[...]) * cos + (kpar * gkp_ref[...]) * sin
    kr_ref[...] = kr[:, 0:QK_ROPE]
    krb_ref[...] = kr.astype(BF16)


def _mla_proj(x, gain, cos_t, sin_t, weights, bm):
    n, d = x.shape
    nh = N_HEADS
    n_tab = cos_t.shape[0] // bm
    outs = [jax.ShapeDtypeStruct((n, KV_LORA), F32), jax.ShapeDtypeStruct((n, QK_ROPE), F32),
            jax.ShapeDtypeStruct((n, nh), F32), jax.ShapeDtypeStruct((n, KV_LORA), BF16),
            jax.ShapeDtypeStruct((n, LANES), BF16), jax.ShapeDtypeStruct((nh, n, KV_LORA), BF16),
            jax.ShapeDtypeStruct((nh, n, LANES), BF16)]

    def row(w):
        return pl.BlockSpec((bm, w), lambda m: (m, 0))

    tab = pl.BlockSpec((bm, LANES), lambda m: (m % n_tab, 0))
    return pl.pallas_call(
        _mla_proj_body,
        grid=(n // bm,),
        in_specs=[row(d), _resident(gain.shape), tab, tab] + [_resident(w.shape) for w in weights],
        out_specs=[row(KV_LORA), row(QK_ROPE), row(nh), row(KV_LORA), row(LANES),
                   pl.BlockSpec((nh, bm, KV_LORA), lambda m: (0, m, 0)),
                   pl.BlockSpec((nh, bm, LANES), lambda m: (0, m, 0))],
        out_shape=outs,
        compiler_params=_params(("parallel",)),
        name="mla_proj",
    )(x, gain, cos_t, sin_t, *weights)


HEADS_PER_STEP = 4


def _attn_p_body(ql_ref, qr_ref, kc_ref, kr_ref, kst_ref, o_ref, m_ref, l_ref, acc_ref, s_ref, *, bq, bk):
    hg = pl.program_id(1)
    qi = pl.program_id(2)
    ki = pl.program_id(3) - (pl.num_programs(3) - 1 - qi)
    nhs = ql_ref.shape[0]

    @pl.when(ki == 0)
    def _():
        m_ref[...] = jnp.full(m_ref.shape, NEG, F32)
        l_ref[...] = jnp.zeros(l_ref.shape, F32)
        acc_ref[...] = jnp.zeros(acc_ref.shape, F32)

    def step(keep):
        kc = kc_ref[...]
        kr = kr_ref[...]
        for i in range(nhs):
            s = _dot_nt(ql_ref[i], kc) + _dot_nt(qr_ref[i], kr)
            s = s * (kst_ref[pl.ds(hg * nhs + i, 1), :] * (ATTN_SCALE * LOG2E))
            s_ref[i * bq:(i + 1) * bq, :] = s if keep is None else jnp.where(keep, s, NEG)
        s = s_ref[...]
        m_prev = m_ref[...]
        m_new = jnp.maximum(m_prev, jnp.max(s, axis=-1, keepdims=True))
        alpha = jnp.exp2(m_prev - m_new)
        p = jnp.exp2(s - m_new)
        l_ref[...] = alpha * l_ref[...] + jnp.sum(p, axis=-1, keepdims=True)
        acc_ref[...] = alpha * acc_ref[...] + _dot(p.astype(BF16), kc)
        m_ref[...] = m_new

    @pl.when(jnp.logical_and(ki >= 0, ki < qi))
    def _():
        step(None)

    @pl.when(ki == qi)
    def _():
        step(lax.broadcasted_iota(jnp.int32, (bq, bk), 1) <= lax.broadcasted_iota(jnp.int32, (bq, bk), 0))
        o = acc_ref[...] / l_ref[...]
        for i in range(nhs):
            o_ref[i] = o[i * bq:(i + 1) * bq, :].astype(BF16)


def _attn_prompt(ql, qr, kc, kr, kst, bq):
    nh, b, t, c = ql.shape
    bk = bq
    nhs = HEADS_PER_STEP
    nk = t // bk

    def kblk(q, k):
        return jnp.maximum(k - (nk - 1 - q), 0)

    body = functools.partial(_attn_p_body, bq=bq, bk=bk)
    return pl.pallas_call(
        body,
        grid=(b, nh // nhs, t // bq, t // bk),
        in_specs=[pl.BlockSpec((nhs, None, bq, c), lambda i, h, q, k: (h, i, q, 0)),
                  pl.BlockSpec((nhs, None, bq, LANES), lambda i, h, q, k: (h, i, q, 0)),
                  pl.BlockSpec((None, bk, c), lambda i, h, q, k: (i, kblk(q, k), 0)),
                  pl.BlockSpec((None, bk, LANES), lambda i, h, q, k: (i, kblk(q, k), 0)),
                  pl.BlockSpec((None, nh, bk), lambda i, h, q, k: (i, 0, kblk(q, k)))],
        out_specs=pl.BlockSpec((nhs, None, bq, c), lambda i, h, q, k: (h, i, q, 0)),
        out_shape=jax.ShapeDtypeStruct((nh, b, t, c), BF16),
        scratch_shapes=[pltpu.VMEM((nhs * bq, 1), F32), pltpu.VMEM((nhs * bq, 1), F32),
                        pltpu.VMEM((nhs * bq, c), F32), pltpu.VMEM((nhs * bq, bk), F32)],
        compiler_params=_params(("parallel", "parallel", "parallel", "arbitrary")),
        name="attn_prompt",
    )(ql, qr, kc, kr, kst)


def _attn_s_body(pt_ref, ql_ref, qr_ref, kcs_ref, krs_ref, kss_ref, cc_hbm, crt_hbm, cst_hbm, o_ref,
                 cbuf, rbuf, sbuf, kcb, sem, *, n_chunks):
    nh = ql_ref.shape[0]
    b = pl.program_id(0)
    total = pl.num_programs(0) * n_chunks

    def copies(g, slot):
        seq = g // n_chunks
        base = (g % n_chunks) * PAGES_PER_CHUNK
        out = []
        for i in range(PAGES_PER_CHUNK):
            page = pt_ref[seq, base + i]
            keys = pl.ds(i * PAGE_SIZE, PAGE_SIZE)
            out.append(pltpu.make_async_copy(cc_hbm.at[page], cbuf.at[slot, keys], sem.at[0, slot]))
            out.append(pltpu.make_async_copy(crt_hbm.at[page], rbuf.at[slot, :, keys], sem.at[1, slot]))
            out.append(pltpu.make_async_copy(cst_hbm.at[page], sbuf.at[slot, :, keys], sem.at[2, slot]))
        return out

    @pl.when(b == 0)
    def _():
        for g0 in range(CHUNK_SLOTS - 1):
            for cp in copies(g0, g0):
                cp.start()

    ql = ql_ref[...]
    qr = qr_ref[...]
    qr_rope = qr[:, 0:QK_ROPE]

    kc_self = kcs_ref[...]
    s_self = (jnp.sum(ql.astype(F32) * kc_self.astype(F32), axis=-1, keepdims=True)
              + jnp.sum(qr.astype(F32) * krs_ref[...].astype(F32), axis=-1, keepdims=True))
    m0 = s_self * (kss_ref[...] * (ATTN_SCALE * LOG2E))
    l0 = jnp.ones((nh, 1), F32)
    acc0 = jnp.broadcast_to(kc_self.astype(F32), (nh, KV_LORA))

    def step(c, carry):
        m_prev, l_prev, acc = carry
        g = b * n_chunks + c
        slot = g % CHUNK_SLOTS
        ahead = g + (CHUNK_SLOTS - 1)

        @pl.when(ahead < total)
        def _():
            for cp in copies(ahead, ahead % CHUNK_SLOTS):
                cp.start()

        for cp in copies(g, slot):
            cp.wait()

        kcb[...] = cbuf[slot].astype(BF16)
        kc = kcb[...]
        s = (_dot_nt(ql, kc) + _dot(qr_rope, rbuf[slot].astype(BF16))) * (sbuf[slot] * (ATTN_SCALE * LOG2E))
        m_new = jnp.maximum(m_prev, jnp.max(s, axis=-1, keepdims=True))
        alpha = jnp.exp2(m_prev - m_new)
        p = jnp.exp2(s - m_new)
        l_new = alpha * l_prev + jnp.sum(p, axis=-1, keepdims=True)
        acc_new = alpha * acc + _dot(p.astype(BF16), kc)
        return m_new, l_new, acc_new

    _, l_fin, acc_fin = lax.fori_loop(0, n_chunks, step, (m0, l0, acc0))
    o_ref[...] = (acc_fin / l_fin).astype(BF16)


def _attn_sample(page_table, ql, qr, kc_self, kr_self, ks_self, cache_c, cache_r, cache_s):
    n, nh, c = ql.shape
    n_chunks = page_table.shape[1] // PAGES_PER_CHUNK
    assert n * n_chunks >= CHUNK_SLOTS - 1
    rows = PAGES_PER_CHUNK * PAGE_SIZE
    body = functools.partial(_attn_s_body, n_chunks=n_chunks)
    any_spec = pl.BlockSpec(memory_space=pl.ANY)
    grid_spec = pltpu.PrefetchScalarGridSpec(
        num_scalar_prefetch=1,
        grid=(n,),
        in_specs=[pl.BlockSpec((None, nh, c), lambda i, pt: (i, 0, 0)),
                  pl.BlockSpec((None, nh, LANES), lambda i, pt: (i, 0, 0)),
                  pl.BlockSpec((None, 1, c), lambda i, pt: (i, 0, 0)),
                  pl.BlockSpec((None, 1, LANES), lambda i, pt: (i, 0, 0)),
                  pl.BlockSpec((None, nh, 1), lambda i, pt: (i, 0, 0)),
                  any_spec, any_spec, any_spec],
        out_specs=pl.BlockSpec((None, nh, c), lambda i, pt: (i, 0, 0)),
        scratch_shapes=[pltpu.VMEM((CHUNK_SLOTS, rows, c), F32), pltpu.VMEM((CHUNK_SLOTS, QK_ROPE, rows), F32),
                        pltpu.VMEM((CHUNK_SLOTS, nh, rows), F32), pltpu.VMEM((rows, c), BF16),
                        pltpu.SemaphoreType.DMA((3, CHUNK_SLOTS))],
    )
    return pl.pallas_call(
        body,
        grid_spec=grid_spec,
        out_shape=jax.ShapeDtypeStruct((n, nh, c), BF16),
        compiler_params=_params(("arbitrary",)),
        name="attn_sample",
    )(page_table, ql, qr, kc_self, kr_self, ks_self, cache_c, cache_r, cache_s)


def _mla_out_body(ol_ref, x_ref, wuv_ref, wo_ref, o_ref, ov_ref):
    nh = ol_ref.shape[0]
    for i in range(nh):
        ov_ref[:, i * LANES:(i + 1) * LANES] = _dot(ol_ref[i], wuv_ref[i]).astype(BF16)
    o_ref[...] = x_ref[...] + _dot(ov_ref[...], wo_ref[...])


def _mla_out(o_lat, x, w_uv_h, w_o, bm):
    nh, n, c = o_lat.shape
    d = x.shape[1]
    return pl.pallas_call(
        _mla_out_body,
        grid=(n // bm,),
        in_specs=[pl.BlockSpec((nh, bm, c), lambda m: (0, m, 0)),
                  pl.BlockSpec((bm, d), lambda m: (m, 0)),
                  _resident(w_uv_h.shape), _resident(w_o.shape)],
        out_specs=pl.BlockSpec((bm, d), lambda m: (m, 0)),
        out_shape=jax.ShapeDtypeStruct((n, d), F32),
        scratch_shapes=[pltpu.VMEM((bm, nh * LANES), BF16)],
        compiler_params=_params(("parallel",)),
        name="mla_out",
    )(o_lat, x, w_uv_h, w_o)


def _rope_tables(pos):
    half = QK_ROPE // 2
    inv_freq = ROPE_THETA ** (-jnp.arange(half, dtype=F32) / half)
    ang = pos.astype(F32)[:, None] * inv_freq[None, :]
    cos, sin = jnp.cos(ang), jnp.sin(ang)
    z = jnp.zeros((pos.shape[0], LANES - QK_ROPE), F32)
    return jnp.concatenate([cos, cos, z], axis=1), jnp.concatenate([-sin, sin, z], axis=1)


def _swap_halves(a):
    half = QK_ROPE // 2
    return jnp.concatenate([a[..., half:], a[..., :half]], axis=-1)


def _pad_rope(a):
    return jnp.concatenate([a, jnp.zeros(a.shape[:-1] + (LANES - QK_ROPE,), a.dtype)], axis=-1)


def _mla_weights(w_dq, q_norm, w_uq, w_dkv, kv_norm, w_uk, gain_q, gain_k):
    nh = N_HEADS
    lq = w_uq.shape[0]
    w_rope = w_uq[:, :, QK_NOPE:]
    per_head = lambda a: a.reshape(lq, nh * LANES).astype(BF16)
    tile = lambda v: jnp.tile(v, nh)[None, :]
    head_of_lane = jnp.arange(nh * LANES) // LANES
    sel = (head_of_lane[:, None] == jnp.arange(nh)[None, :]).astype(BF16)
    w_kr = w_dkv[:, KV_LORA:]
    return [
        w_dq.astype(BF16), q_norm[None, :],
        per_head(w_uq[:, :, :QK_NOPE]), per_head(_pad_rope(w_rope)), per_head(_pad_rope(_swap_halves(w_rope))),
        tile(gain_q[:QK_NOPE]), tile(_pad_rope(gain_q[QK_NOPE:])), tile(_pad_rope(_swap_halves(gain_q[QK_NOPE:]))),
        tile(gain_k[:QK_NOPE]),
        w_dkv[:, :KV_LORA].astype(BF16), kv_norm[None, :],
        _pad_rope(w_kr).astype(BF16), _pad_rope(_swap_halves(w_kr)).astype(BF16),
        _pad_rope(gain_k[QK_NOPE:])[None, :], _pad_rope(_swap_halves(gain_k[QK_NOPE:]))[None, :],
        w_uk.reshape(w_uk.shape[0], nh * QK_NOPE).astype(BF16),
        jnp.transpose(w_uk, (1, 2, 0)).astype(BF16),
        sel, jnp.transpose(sel), jnp.ones((LANES, nh), BF16),
    ]


def _tile_rows(n, pref):
    return pref if n % pref == 0 else n


def kernel(x_prompt, x_sample, p_prompt, p_sample, state_conv_a, state_conv_b, cache_ckv, cache_krope,
           cache_kscale, page_table, norm_gains, ffn_w_gu, ffn_w_down, ple_w_gate, ple_w_proj, conv_w_in,
           conv_dw_a, conv_dw_a_bias, conv_ln_gain, conv_ln_bias, conv_dw_b, conv_w_out, mla_w_dq, mla_q_norm,
           mla_w_uq, mla_w_dkv, mla_kv_norm, mla_w_uk, mla_w_uv, mla_qk_gain_q, mla_qk_gain_k, mla_w_o):
    bsz, seq, d = x_prompt.shape
    dec = x_sample.shape[0]
    depth = norm_gains.shape[0]
    nh = N_HEADS
    past_len = page_table.shape[1] * PAGE_SIZE
    n_p = bsz * seq

    xp = x_prompt.reshape(n_p, d)
    xs = x_sample.reshape(dec, d)
    bm_p = _tile_rows(n_p, 512)
    bm_ffn = _tile_rows(n_p, 1024)
    bf = _tile_rows(ffn_w_down.shape[2], 512)
    w_gu = ffn_w_gu.astype(BF16)
    w_down = ffn_w_down.astype(BF16)
    w_gate = ple_w_gate.astype(BF16)
    w_proj = ple_w_proj.astype(BF16)
    pp = p_prompt.reshape(depth, n_p, -1)
    ps = p_sample.reshape(depth, dec, -1)

    cos_p, sin_p = _rope_tables(jnp.arange(seq, dtype=jnp.int32))
    cos_s, sin_s = _rope_tables(jnp.full((dec,), past_len, jnp.int32))

    ca_p, ca_s, cb_p, cb_s = [], [], [], []
    ckv_p, ckv_s, kr_p, kr_s, ks_p, ks_s = [], [], [], [], [], []
    for i in range(depth):
        j = i // 2
        gains = norm_gains[i][:, None, :]

        xp = _ffn(xp, gains[0], w_gu, w_down, i, 0, bm_ffn, bf)
        xs = _ffn(xs, gains[0], w_gu, w_down, i, 0, dec, bf)

        if i % 2 == 0:
            w_in = conv_w_in[j].astype(BF16)
            w_out = conv_w_out[j].astype(BF16)
            dwa = jnp.concatenate([conv_dw_a[j], jnp.zeros((1, CONV_WIDTH), F32)], axis=0)
            dwb = jnp.concatenate([conv_dw_b[j], jnp.zeros((8 - KB, CONV_WIDTH), F32)], axis=0)
            ba, lng, lnb = conv_dw_a_bias[j][None, :], conv_ln_gain[j][None, :], conv_ln_bias[j][None, :]

            ua, ub, gb = _conv_in(xp, gains[1], w_in, bm_p, 512)
            ua3, ub3 = ua.reshape(bsz, seq, CONV_WIDTH), ub.reshape(bsz, seq, CONV_WIDTH)
            xp = _conv_prompt(ua3, ub3, gb.reshape(bsz, seq, CONV_WIDTH), xp.reshape(bsz, seq, d),
                              dwa, ba, lng, lnb, dwb, w_out, _tile_rows(seq, 512)).reshape(n_p, d)
            ca_p.append(ua3[:, seq - (KA - 1):])
            cb_p.append(ub3[:, seq - (KB - 1):])

            ua, ub, gb = _conv_in(xs, gains[1], w_in, dec, 512)
            sa, sb = state_conv_a[j], state_conv_b[j]
            xs = _conv_sample(jnp.swapaxes(sa, 0, 1), ua, jnp.swapaxes(sb, 0, 1), ub, gb, xs,
                              dwa, ba, lng, lnb, dwb, w_out)
            ca_s.append(jnp.concatenate([sa[:, 1:], ua[:, None, :]], axis=1))
            cb_s.append(jnp.concatenate([sb[:, 1:], ub[:, None, :]], axis=1))
        else:
            weights = _mla_weights(mla_w_dq[j], mla_q_norm[j], mla_w_uq[j], mla_w_dkv[j], mla_kv_norm[j],
                                   mla_w_uk[j], mla_qk_gain_q[j], mla_qk_gain_k[j])
            w_uv_h = jnp.transpose(mla_w_uv[j], (1, 0, 2)).astype(BF16)
            w_o = mla_w_o[j].astype(BF16)

            bmq = _tile_rows(seq, 256)
            ckv, kr, ks, ckvb, krb, ql, qr = _mla_proj(xp, gains[1], cos_p, sin_p, weights, bmq)
            kst = jnp.swapaxes(ks.reshape(bsz, seq, nh), 1, 2)
            o_lat = _attn_prompt(ql.reshape(nh, bsz, seq, KV_LORA), qr.reshape(nh, bsz, seq, LANES),
                                 ckvb.reshape(bsz, seq, KV_LORA), krb.reshape(bsz, seq, LANES), kst,
                                 _tile_rows(seq, 512))
            xp = _mla_out(o_lat.reshape(nh, n_p, KV_LORA), xp, w_uv_h, w_o, bm_p)
            npg = seq // PAGE_SIZE
            ckv_p.append(ckv.reshape(bsz, npg, PAGE_SIZE, KV_LORA))
            kr_p.append(kr.reshape(bsz, npg, PAGE_SIZE, QK_ROPE))
            ks_p.append(ks.reshape(bsz, npg, PAGE_SIZE, nh))

            ckv, kr, ks, ckvb, krb, ql, qr = _mla_proj(xs, gains[1], cos_s, sin_s, weights, dec)
            o_lat = _attn_sample(page_table, jnp.swapaxes(ql, 0, 1), jnp.swapaxes(qr, 0, 1), ckvb[:, None, :],
                                 krb[:, None, :], ks[:, :, None], cache_ckv[j],
                                 jnp.swapaxes(cache_krope[j], 1, 2), jnp.swapaxes(cache_kscale[j], 1, 2))
            xs = _mla_out(jnp.swapaxes(o_lat, 0, 1), xs, w_uv_h, w_o, dec)
            ckv_s.append(ckv[:, None, :])
            kr_s.append(kr[:, None, :])
            ks_s.append(ks[:, None, :])

        xp = _ffn(xp, gains[2], w_gu, w_down, i, 1, bm_ffn, bf)
        xs = _ffn(xs, gains[2], w_gu, w_down, i, 1, dec, bf)
        xp = _ple(xp, gains[3], w_gate, pp, w_proj, i, bm_p)
        xs = _ple(xs, gains[3], w_gate, ps, w_proj, i, dec)

    return (xp.reshape(bsz, seq, d), xs.reshape(dec, 1, d),
            jnp.stack(ca_p), jnp.stack(ca_s), jnp.stack(cb_p), jnp.stack(cb_s),
            jnp.stack(ckv_p), jnp.stack(ckv_s), jnp.stack(kr_p), jnp.stack(kr_s),
            jnp.stack(ks_p), jnp.stack(ks_s))
```

```python
import functools

import jax
import jax.numpy as jnp
from jax import lax
from jax.experimental import pallas as pl
from jax.experimental.pallas import tpu as pltpu

F32 = jnp.float32
BF16 = jnp.bfloat16

EPS = 1e-6
ROPE_THETA = 10000.0
NEG = -1e30
PAGE_SIZE = 128
N_HEADS = 16
QK_NOPE = 128
QK_ROPE = 64
QK_DIM = QK_NOPE + QK_ROPE
KV_LORA = 512
KA = 31
KB = 3
CONV_WIDTH = 1024
ATTN_SCALE = QK_DIM ** -0.5
LOG2E = 1.4426950408889634
LANES = 128
SUBLANES = 8
VMEM_LIMIT = 56 * 1024 * 1024
PAGES_PER_CHUNK = 32
CHUNK_SLOTS = 3


def _params(semantics):
    return pltpu.CompilerParams(dimension_semantics=semantics, vmem_limit_bytes=VMEM_LIMIT)


def _resident(shape):
    nd = len(shape)
    return pl.BlockSpec(shape, lambda *_: (0,) * nd, pipeline_mode=pl.Buffered(1))


def _rms(x, g):
    return x * lax.rsqrt(jnp.mean(x * x, axis=-1, keepdims=True) + EPS) * g


def _dot(a, b):
    return jnp.dot(a, b, preferred_element_type=F32)


def _dot_nt(a, b):
    return lax.dot_general(a, b, (((1,), (1,)), ((), ())), preferred_element_type=F32)


def _hl_dot(v, sel):
    hi = v.astype(BF16)
    lo = (v - hi.astype(F32)).astype(BF16)
    return _dot(hi, sel) + _dot(lo, sel)


def _ffn_body(x_ref, g_ref, wg_ref, wu_ref, wd_ref, o_ref, h_ref):
    @pl.when(pl.program_id(1) == 0)
    def _():
        x = x_ref[...]
        h_ref[...] = _rms(x, g_ref[...]).astype(BF16)
        o_ref[...] = x

    h = h_ref[...]
    g = _dot(h, wg_ref[...])
    u = _dot(h, wu_ref[...])
    a = (0.5 * g * jax.nn.sigmoid(g) * u).astype(BF16)
    o_ref[...] += _dot(a, wd_ref[...])


def _ffn(x, gain, w_gu, w_down, layer, which, bm, bf):
    n, d = x.shape
    nf = w_down.shape[2] // bf
    return pl.pallas_call(
        _ffn_body,
        grid=(n // bm, nf),
        in_specs=[pl.BlockSpec((bm, d), lambda m, f: (m, 0)),
                  pl.BlockSpec((1, d), lambda m, f: (0, 0)),
                  pl.BlockSpec((None, None, d, bf), lambda m, f: (layer, which, 0, f)),
                  pl.BlockSpec((None, None, d, bf), lambda m, f: (layer, which, 0, nf + f)),
                  pl.BlockSpec((None, None, bf, d), lambda m, f: (layer, which, f, 0))],
        out_specs=pl.BlockSpec((bm, d), lambda m, f: (m, 0)),
        out_shape=jax.ShapeDtypeStruct((n, d), F32),
        scratch_shapes=[pltpu.VMEM((bm, d), BF16)],
        compiler_params=_params(("parallel", "arbitrary")),
        name="ffn",
    )(x, gain, w_gu, w_gu, w_down)


def _layer_resident(shape, layer):
    nd = len(shape) - 1
    return pl.BlockSpec((None,) + tuple(shape[1:]), lambda *_: (layer,) + (0,) * nd, pipeline_mode=pl.Buffered(1))


def _ple_body(x_ref, g_ref, wg_ref, p_ref, wp_ref, o_ref):
    x = x_ref[...]
    h = _rms(x, g_ref[...]).astype(BF16)
    gate = jax.nn.sigmoid(_dot(h, wg_ref[...]))
    o_ref[...] = x + gate * _dot(p_ref[...].astype(BF16), wp_ref[...])


def _ple(x, gain, w_gate, p, w_proj, layer, bm):
    n, d = x.shape
    pd = p.shape[2]
    return pl.pallas_call(
        _ple_body,
        grid=(n // bm,),
        in_specs=[pl.BlockSpec((bm, d), lambda m: (m, 0)),
                  _resident((1, d)),
                  _layer_resident(w_gate.shape, layer),
                  pl.BlockSpec((None, bm, pd), lambda m: (layer, m, 0)),
                  _layer_resident(w_proj.shape, layer)],
        out_specs=pl.BlockSpec((bm, d), lambda m: (m, 0)),
        out_shape=jax.ShapeDtypeStruct((n, d), F32),
        compiler_params=_params(("parallel",)),
        name="ple",
    )(x, gain, w_gate, p, w_proj)


def _conv_in_body(x_ref, g_ref, wav_ref, wag_ref, wbh_ref, wbb_ref, wbc_ref, ua_ref, ub_ref, gb_ref, h_ref):
    @pl.when(pl.program_id(1) == 0)
    def _():
        h_ref[...] = _rms(x_ref[...], g_ref[...]).astype(BF16)

    h = h_ref[...]
    ua_ref[...] = _dot(h, wav_ref[...]) * jax.nn.sigmoid(_dot(h, wag_ref[...]))
    ub_ref[...] = _dot(h, wbc_ref[...]) * _dot(h, wbh_ref[...])
    gb_ref[...] = _dot(h, wbb_ref[...])


def _conv_in(x, gain, w_in, bm, bc):
    n, d = x.shape
    c = CONV_WIDTH
    nc = c // bc

    def wspec(k):
        return pl.BlockSpec((d, bc), lambda m, j: (0, k * nc + j))

    out = jax.ShapeDtypeStruct((n, c), F32)
    ospec = pl.BlockSpec((bm, bc), lambda m, j: (m, j))
    return pl.pallas_call(
        _conv_in_body,
        grid=(n // bm, nc),
        in_specs=[pl.BlockSpec((bm, d), lambda m, j: (m, 0)),
                  pl.BlockSpec((1, d), lambda m, j: (0, 0)),
                  wspec(0), wspec(1), wspec(2), wspec(3), wspec(4)],
        out_specs=[ospec, ospec, ospec],
        out_shape=[out, out, out],
        scratch_shapes=[pltpu.VMEM((bm, d), BF16)],
        compiler_params=_params(("parallel", "arbitrary")),
        name="conv_in",
    )(x, gain, w_in, w_in, w_in, w_in, w_in)


def _ln_silu(y, g, b):
    yc = y - jnp.mean(y, axis=-1, keepdims=True)
    z = yc * lax.rsqrt(jnp.mean(yc * yc, axis=-1, keepdims=True) + EPS) * g + b
    return z * jax.nn.sigmoid(z)


HALO_A = 32
HALO_B = 8
CONV_ROWS = 64


def _conv_p_body(ua_ref, uah_ref, ub_ref, ubh_ref, gb_ref, x_ref, dwa_ref, ba_ref, lng_ref, lnb_ref, dwb_ref,
                 wo_ref, o_ref, pa_ref, pb_ref, ya_ref, yc_ref, sh_ref, *, bt):
    first = pl.program_id(1) == 0
    pa_ref[0:HALO_A, :] = jnp.where(first, 0.0, uah_ref[...])
    pa_ref[HALO_A:HALO_A + bt, :] = ua_ref[...]
    pb_ref[0:HALO_B, :] = jnp.where(first, 0.0, ubh_ref[...])
    pb_ref[HALO_B:HALO_B + bt, :] = ub_ref[...]
    c = pa_ref.shape[1]
    sh_rows = sh_ref.shape[1]

    def chunk(cc, carry):
        cs = pl.multiple_of(cc * LANES, LANES)
        wa = dwa_ref[:, pl.ds(cs, LANES)]
        wb = dwb_ref[:, pl.ds(cs, LANES)]
        for s in range(1, SUBLANES):
            sh_ref[s - 1] = pa_ref[pl.ds(s, sh_rows), pl.ds(cs, LANES)]
        for r in range(bt // CONV_ROWS):
            acc = jnp.zeros((CONV_ROWS, LANES), F32)
            for k in range(KA):
                off = HALO_A - (KA - 1) + k
                s = off % SUBLANES
                base = r * CONV_ROWS + off - s
                if s == 0:
                    tap = pa_ref[pl.ds(base, CONV_ROWS), pl.ds(cs, LANES)]
                else:
                    tap = sh_ref[s - 1, pl.ds(base, CONV_ROWS), :]
                acc = acc + wa[k:k + 1, :] * tap
            ya_ref[pl.ds(r * CONV_ROWS, CONV_ROWS), pl.ds(cs, LANES)] = acc
            accb = jnp.zeros((CONV_ROWS, LANES), F32)
            for k in range(KB):
                accb = accb + wb[k:k + 1, :] * pb_ref[pl.ds(r * CONV_ROWS + HALO_B - (KB - 1) + k, CONV_ROWS),
                                                      pl.ds(cs, LANES)]
            yb = gb_ref[pl.ds(r * CONV_ROWS, CONV_ROWS), pl.ds(cs, LANES)] * accb
            yc_ref[pl.ds(r * CONV_ROWS, CONV_ROWS), pl.ds(c + cs, LANES)] = yb.astype(BF16)
        return carry

    lax.fori_loop(0, c // LANES, chunk, 0)
    ya = _ln_silu(ya_ref[...] + ba_ref[...], lng_ref[...], lnb_ref[...])
    yc_ref[:, 0:c] = ya.astype(BF16)
    o_ref[...] = x_ref[...] + _dot(yc_ref[...], wo_ref[...])


def _conv_prompt(ua, ub, gb, x, dwa, ba, lng, lnb, dwb, w_out, bt):
    b, t, c = ua.shape
    d = x.shape[2]
    body = functools.partial(_conv_p_body, bt=bt)
    cur = pl.BlockSpec((None, bt, c), lambda i, j: (i, j, 0))
    return pl.pallas_call(
        body,
        grid=(b, t // bt),
        in_specs=[cur,
                  pl.BlockSpec((None, HALO_A, c), lambda i, j: (i, jnp.maximum(j * (bt // HALO_A) - 1, 0), 0)),
                  cur,
                  pl.BlockSpec((None, HALO_B, c), lambda i, j: (i, jnp.maximum(j * (bt // HALO_B) - 1, 0), 0)),
                  cur,
                  pl.BlockSpec((None, bt, d), lambda i, j: (i, j, 0)),
                  _resident(dwa.shape), _resident(ba.shape), _resident(lng.shape), _resident(lnb.shape),
                  _resident(dwb.shape), _resident(w_out.shape)],
        out_specs=pl.BlockSpec((None, bt, d), lambda i, j: (i, j, 0)),
        out_shape=jax.ShapeDtypeStruct((b, t, d), F32),
        scratch_shapes=[pltpu.VMEM((HALO_A + bt, c), F32), pltpu.VMEM((HALO_B + bt, c), F32),
                        pltpu.VMEM((bt, c), F32), pltpu.VMEM((bt, 2 * c), BF16),
                        pltpu.VMEM((SUBLANES - 1, bt + HALO_A - SUBLANES, LANES), F32)],
        compiler_params=_params(("parallel", "arbitrary")),
        name="conv_prompt",
    )(ua, ua, ub, ub, gb, x, dwa, ba, lng, lnb, dwb, w_out)


def _conv_s_body(sa_ref, ua_ref, sb_ref, ub_ref, gb_ref, x_ref, dwa_ref, ba_ref, lng_ref, lnb_ref, dwb_ref,
                 wo_ref, o_ref):
    ya = dwa_ref[KA - 1:KA, :] * ua_ref[...]
    for k in range(KA - 1):
        ya = ya + dwa_ref[k:k + 1, :] * sa_ref[k]
    yb = dwb_ref[KB - 1:KB, :] * ub_ref[...]
    for k in range(KB - 1):
        yb = yb + dwb_ref[k:k + 1, :] * sb_ref[k]
    ya = _ln_silu(ya + ba_ref[...], lng_ref[...], lnb_ref[...])
    yb = gb_ref[...] * yb
    yc = jnp.concatenate([ya.astype(BF16), yb.astype(BF16)], axis=1)
    o_ref[...] = x_ref[...] + _dot(yc, wo_ref[...])


def _conv_sample(sa_t, ua, sb_t, ub, gb, x, dwa, ba, lng, lnb, dwb, w_out):
    n, d = x.shape
    args = (sa_t, ua, sb_t, ub, gb, x, dwa, ba, lng, lnb, dwb, w_out)
    return pl.pallas_call(
        _conv_s_body,
        grid=(1,),
        in_specs=[_resident(a.shape) for a in args],
        out_specs=pl.BlockSpec((n, d), lambda i: (0, 0)),
        out_shape=jax.ShapeDtypeStruct((n, d), F32),
        compiler_params=_params(("arbitrary",)),
        name="conv_sample",
    )(*args)


def _mla_proj_body(x_ref, g_ref, cos_ref, sin_ref, wdq_ref, qn_ref, wqn_ref, wqr_ref, wqp_ref, gqn_ref, gqr_ref,
                   gqp_ref, gkn_ref, wkc_ref, kvn_ref, wkr_ref, wkp_ref, gkr_ref, gkp_ref, wuk_ref, wukt_ref,
                   sel_ref, selt_ref, ones_ref,
                   ckv_ref, kr_ref, ks_ref, ckvb_ref, krb_ref, ql_ref, qr_ref):
    nh = ql_ref.shape[0]
    h = _rms(x_ref[...], g_ref[...]).astype(BF16)
    cos = cos_ref[...]
    sin = sin_ref[...]
    sel = sel_ref[...]

    cq = _rms(_dot(h, wdq_ref[...]), qn_ref[...]).astype(BF16)
    qn = _dot(cq, wqn_ref[...])
    qr = _dot(cq, wqr_ref[...])
    qp = _dot(cq, wqp_ref[...])
    ssq = _hl_dot(qn * qn + qr * qr, sel)
    rq = _hl_dot(lax.rsqrt(ssq * (1.0 / QK_DIM) + EPS), selt_ref[...])
    cos_h = jnp.tile(cos, (1, nh))
    sin_h = jnp.tile(sin, (1, nh))
    qrot = (qr * rq * gqr_ref[...]) * cos_h + (qp * rq * gqp_ref[...]) * sin_h
    qnn = (qn * rq * gqn_ref[...]) * gkn_ref[...]
    for i in range(nh):
        sl = slice(i * LANES, (i + 1) * LANES)
        ql_ref[i] = _dot(qnn[:, sl].astype(BF16), wukt_ref[i]).astype(BF16)
        qr_ref[i] = qrot[:, sl].astype(BF16)

    ckv = _rms(_dot(h, wkc_ref[...]), kvn_ref[...])
    ckv_ref[...] = ckv
    ckvb = ckv.astype(BF16)
    ckvb_ref[...] = ckvb
    kraw = _dot(h, wkr_ref[...])
    kpar = _dot(h, wkp_ref[...])
    kn = _dot(ckvb, wuk_ref[...])
    ssk = _hl_dot(kn * kn, sel) + _hl_dot(kraw * kraw, ones_ref[...])
    ks_ref[...] = lax.rsqrt(ssk * (1.0 / QK_DIM) + EPS)
    kr = (kraw * gkr_ref[...]) * cos + (kpar * gkp_ref[...]) * sin
    kr_ref[...] = kr[:, 0:QK_ROPE]
    krb_ref[...] = kr.astype(BF16)


def _mla_proj(x, gain, cos_t, sin_t, weights, bm):
    n, d = x.shape
    nh = N_HEADS
    n_tab = cos_t.shape[0] // bm
    outs = [jax.ShapeDtypeStruct((n, KV_LORA), F32), jax.ShapeDtypeStruct((n, QK_ROPE), F32),
            jax.ShapeDtypeStruct((n, nh), F32), jax.ShapeDtypeStruct((n, KV_LORA), BF16),
            jax.ShapeDtypeStruct((n, LANES), BF16), jax.ShapeDtypeStruct((nh, n, KV_LORA), BF16),
            jax.ShapeDtypeStruct((nh, n, LANES), BF16)]

    def row(w):
        return pl.BlockSpec((bm, w), lambda m: (m, 0))

    tab = pl.BlockSpec((bm, LANES), lambda m: (m % n_tab, 0))
    return pl.pallas_call(
        _mla_proj_body,
        grid=(n // bm,),
        in_specs=[row(d), _resident(gain.shape), tab, tab] + [_resident(w.shape) for w in weights],
        out_specs=[row(KV_LORA), row(QK_ROPE), row(nh), row(KV_LORA), row(LANES),
                   pl.BlockSpec((nh, bm, KV_LORA), lambda m: (0, m, 0)),
                   pl.BlockSpec((nh, bm, LANES), lambda m: (0, m, 0))],
        out_shape=outs,
        compiler_params=_params(("parallel",)),
        name="mla_proj",
    )(x, gain, cos_t, sin_t, *weights)


HEADS_PER_STEP = 4


def _attn_p_body(ql_ref, qr_ref, kc_ref, kr_ref, kst_ref, o_ref, m_ref, l_ref, acc_ref, s_ref, *, bq, bk):
    hg = pl.program_id(1)
    qi = pl.program_id(2)
    last = (qi * bq + bq - 1) // bk
    ki = pl.program_id(3) - (pl.num_programs(3) - 1 - last)
    nhs = ql_ref.shape[0]

    @pl.when(ki == 0)
    def _():
        m_ref[...] = jnp.full(m_ref.shape, NEG, F32)
        l_ref[...] = jnp.zeros(l_ref.shape, F32)
        acc_ref[...] = jnp.zeros(acc_ref.shape, F32)

    def step(keep):
        kc = kc_ref[...]
        kr = kr_ref[...]
        for i in range(nhs):
            s = _dot_nt(ql_ref[i], kc) + _dot_nt(qr_ref[i], kr)
            s = s * (kst_ref[pl.ds(hg * nhs + i, 1), :] * (ATTN_SCALE * LOG2E))
            s_ref[i * bq:(i + 1) * bq, :] = s if keep is None else jnp.where(keep, s, NEG)
        s = s_ref[...]
        m_prev = m_ref[...]
        m_new = jnp.maximum(m_prev, jnp.max(s, axis=-1, keepdims=True))
        alpha = jnp.exp2(m_prev - m_new)
        p = jnp.exp2(s - m_new)
        l_ref[...] = alpha * l_ref[...] + jnp.sum(p, axis=-1, keepdims=True)
        acc_ref[...] = alpha * acc_ref[...] + _dot(p.astype(BF16), kc)
        m_ref[...] = m_new

    @pl.when(jnp.logical_and(ki >= 0, ki < last))
    def _():
        step(None)

    @pl.when(ki == last)
    def _():
        step(ki * bk + lax.broadcasted_iota(jnp.int32, (bq, bk), 1) <= qi * bq + lax.broadcasted_iota(jnp.int32, (bq, bk), 0))
        o = acc_ref[...] / l_ref[...]
        for i in range(nhs):
            o_ref[i] = o[i * bq:(i + 1) * bq, :].astype(BF16)


def _attn_prompt(ql, qr, kc, kr, kst, bq):
    nh, b, t, c = ql.shape
    bk = 2 * bq
    nhs = 4
    nk = t // bk

    def kblk(q, k):
        return jnp.maximum(k - (nk - 1 - (q * bq + bq - 1) // bk), 0)

    body = functools.partial(_attn_p_body, bq=bq, bk=bk)
    return pl.pallas_call(
        body,
        grid=(b, nh // nhs, t // bq, t // bk),
        in_specs=[pl.BlockSpec((nhs, None, bq, c), lambda i, h, q, k: (h, i, q, 0)),
                  pl.BlockSpec((nhs, None, bq, LANES), lambda i, h, q, k: (h, i, q, 0)),
                  pl.BlockSpec((None, bk, c), lambda i, h, q, k: (i, kblk(q, k), 0)),
                  pl.BlockSpec((None, bk, LANES), lambda i, h, q, k: (i, kblk(q, k), 0)),
                  pl.BlockSpec((None, nh, bk), lambda i, h, q, k: (i, 0, kblk(q, k)))],
        out_specs=pl.BlockSpec((nhs, None, bq, c), lambda i, h, q, k: (h, i, q, 0)),
        out_shape=jax.ShapeDtypeStruct((nh, b, t, c), BF16),
        scratch_shapes=[pltpu.VMEM((nhs * bq, 1), F32), pltpu.VMEM((nhs * bq, 1), F32),
                        pltpu.VMEM((nhs * bq, c), F32), pltpu.VMEM((nhs * bq, bk), F32)],
        compiler_params=_params(("parallel", "parallel", "parallel", "arbitrary")),
        name="attn_prompt",
    )(ql, qr, kc, kr, kst)


def _attn_s_body(pt_ref, ql_ref, qr_ref, kcs_ref, krs_ref, kss_ref, cc_hbm, crt_hbm, cst_hbm, o_ref,
                 cbuf, rbuf, sbuf, kcb, sem, *, n_chunks):
    nh = ql_ref.shape[0]
    b = pl.program_id(0)
    total = pl.num_programs(0) * n_chunks

    def copies(g, slot):
        seq = g // n_chunks
        base = (g % n_chunks) * PAGES_PER_CHUNK
        out = []
        for i in range(PAGES_PER_CHUNK):
            page = pt_ref[seq, base + i]
            keys = pl.ds(i * PAGE_SIZE, PAGE_SIZE)
            out.append(pltpu.make_async_copy(cc_hbm.at[page], cbuf.at[slot, keys], sem.at[0, slot]))
            out.append(pltpu.make_async_copy(crt_hbm.at[page], rbuf.at[slot, :, keys], sem.at[1, slot]))
            out.append(pltpu.make_async_copy(cst_hbm.at[page], sbuf.at[slot, :, keys], sem.at[2, slot]))
        return out

    @pl.when(b == 0)
    def _():
        for g0 in range(CHUNK_SLOTS - 1):
            for cp in copies(g0, g0):
                cp.start()

    ql = ql_ref[...]
    qr = qr_ref[...]
    qr_rope = qr[:, 0:QK_ROPE]

    kc_self = kcs_ref[...]
    s_self = (jnp.sum(ql.astype(F32) * kc_self.astype(F32), axis=-1, keepdims=True)
              + jnp.sum(qr.astype(F32) * krs_ref[...].astype(F32), axis=-1, keepdims=True))
    m0 = s_self * (kss_ref[...] * (ATTN_SCALE * LOG2E))
    l0 = jnp.ones((nh, 1), F32)
    acc0 = jnp.broadcast_to(kc_self.astype(F32), (nh, KV_LORA))

    def step(c, carry):
        m_prev, l_prev, acc = carry
        g = b * n_chunks + c
        slot = g % CHUNK_SLOTS
        ahead = g + (CHUNK_SLOTS - 1)

        @pl.when(ahead < total)
        def _():
            for cp in copies(ahead, ahead % CHUNK_SLOTS):
                cp.start()

        for cp in copies(g, slot):
            cp.wait()

        kcb[...] = cbuf[slot].astype(BF16)
        kc = kcb[...]
        s = (_dot_nt(ql, kc) + _dot(qr_rope, rbuf[slot].astype(BF16))) * (sbuf[slot] * (ATTN_SCALE * LOG2E))
        m_new = jnp.maximum(m_prev, jnp.max(s, axis=-1, keepdims=True))
        alpha = jnp.exp2(m_prev - m_new)
        p = jnp.exp2(s - m_new)
        l_new = alpha * l_prev + jnp.sum(p, axis=-1, keepdims=True)
        acc_new = alpha * acc + _dot(p.astype(BF16), kc)
        return m_new, l_new, acc_new

    _, l_fin, acc_fin = lax.fori_loop(0, n_chunks, step, (m0, l0, acc0))
    o_ref[...] = (acc_fin / l_fin).astype(BF16)


def _attn_sample(page_table, ql, qr, kc_self, kr_self, ks_self, cache_c, cache_r, cache_s):
    n, nh, c = ql.shape
    n_chunks = page_table.shape[1] // PAGES_PER_CHUNK
    assert n * n_chunks >= CHUNK_SLOTS - 1
    rows = PAGES_PER_CHUNK * PAGE_SIZE
    body = functools.partial(_attn_s_body, n_chunks=n_chunks)
    any_spec = pl.BlockSpec(memory_space=pl.ANY)
    grid_spec = pltpu.PrefetchScalarGridSpec(
        num_scalar_prefetch=1,
        grid=(n,),
        in_specs=[pl.BlockSpec((None, nh, c), lambda i, pt: (i, 0, 0)),
                  pl.BlockSpec((None, nh, LANES), lambda i, pt: (i, 0, 0)),
                  pl.BlockSpec((None, 1, c), lambda i, pt: (i, 0, 0)),
                  pl.BlockSpec((None, 1, LANES), lambda i, pt: (i, 0, 0)),
                  pl.BlockSpec((None, nh, 1), lambda i, pt: (i, 0, 0)),
                  any_spec, any_spec, any_spec],
        out_specs=pl.BlockSpec((None, nh, c), lambda i, pt: (i, 0, 0)),
        scratch_shapes=[pltpu.VMEM((CHUNK_SLOTS, rows, c), F32), pltpu.VMEM((CHUNK_SLOTS, QK_ROPE, rows), F32),
                        pltpu.VMEM((CHUNK_SLOTS, nh, rows), F32), pltpu.VMEM((rows, c), BF16),
                        pltpu.SemaphoreType.DMA((3, CHUNK_SLOTS))],
    )
    return pl.pallas_call(
        body,
        grid_spec=grid_spec,
        out_shape=jax.ShapeDtypeStruct((n, nh, c), BF16),
        compiler_params=_params(("arbitrary",)),
        name="attn_sample",
    )(page_table, ql, qr, kc_self, kr_self, ks_self, cache_c, cache_r, cache_s)


def _mla_out_body(ol_ref, x_ref, wuv_ref, wo_ref, o_ref, ov_ref):
    nh = ol_ref.shape[0]
    for i in range(nh):
        ov_ref[:, i * LANES:(i + 1) * LANES] = _dot(ol_ref[i], wuv_ref[i]).astype(BF16)
    o_ref[...] = x_ref[...] + _dot(ov_ref[...], wo_ref[...])


def _mla_out(o_lat, x, w_uv_h, w_o, bm):
    nh, n, c = o_lat.shape
    d = x.shape[1]
    return pl.pallas_call(
        _mla_out_body,
        grid=(n // bm,),
        in_specs=[pl.BlockSpec((nh, bm, c), lambda m: (0, m, 0)),
                  pl.BlockSpec((bm, d), lambda m: (m, 0)),
                  _resident(w_uv_h.shape), _resident(w_o.shape)],
        out_specs=pl.BlockSpec((bm, d), lambda m: (m, 0)),
        out_shape=jax.ShapeDtypeStruct((n, d), F32),
        scratch_shapes=[pltpu.VMEM((bm, nh * LANES), BF16)],
        compiler_params=_params(("parallel",)),
        name="mla_out",
    )(o_lat, x, w_uv_h, w_o)


def _rope_tables(pos):
    half = QK_ROPE // 2
    inv_freq = ROPE_THETA ** (-jnp.arange(half, dtype=F32) / half)
    ang = pos.astype(F32)[:, None] * inv_freq[None, :]
    cos, sin = jnp.cos(ang), jnp.sin(ang)
    z = jnp.zeros((pos.shape[0], LANES - QK_ROPE), F32)
    return jnp.concatenate([cos, cos, z], axis=1), jnp.concatenate([-sin, sin, z], axis=1)


def _swap_halves(a):
    half = QK_ROPE // 2
    return jnp.concatenate([a[..., half:], a[..., :half]], axis=-1)


def _pad_rope(a):
    return jnp.concatenate([a, jnp.zeros(a.shape[:-1] + (LANES - QK_ROPE,), a.dtype)], axis=-1)


def _mla_weights(w_dq, q_norm, w_uq, w_dkv, kv_norm, w_uk, gain_q, gain_k):
    nh = N_HEADS
    lq = w_uq.shape[0]
    w_rope = w_uq[:, :, QK_NOPE:]
    per_head = lambda a: a.reshape(lq, nh * LANES).astype(BF16)
    tile = lambda v: jnp.tile(v, nh)[None, :]
    head_of_lane = jnp.arange(nh * LANES) // LANES
    sel = (head_of_lane[:, None] == jnp.arange(nh)[None, :]).astype(BF16)
    w_kr = w_dkv[:, KV_LORA:]
    return [
        w_dq.astype(BF16), q_norm[None, :],
        per_head(w_uq[:, :, :QK_NOPE]), per_head(_pad_rope(w_rope)), per_head(_pad_rope(_swap_halves(w_rope))),
        tile(gain_q[:QK_NOPE]), tile(_pad_rope(gain_q[QK_NOPE:])), tile(_pad_rope(_swap_halves(gain_q[QK_NOPE:]))),
        tile(gain_k[:QK_NOPE]),
        w_dkv[:, :KV_LORA].astype(BF16), kv_norm[None, :],
        _pad_rope(w_kr).astype(BF16), _pad_rope(_swap_halves(w_kr)).astype(BF16),
        _pad_rope(gain_k[QK_NOPE:])[None, :], _pad_rope(_swap_halves(gain_k[QK_NOPE:]))[None, :],
        w_uk.reshape(w_uk.shape[0], nh * QK_NOPE).astype(BF16),
        jnp.transpose(w_uk, (1, 2, 0)).astype(BF16),
        sel, jnp.transpose(sel), jnp.ones((LANES, nh), BF16),
    ]


def _tile_rows(n, pref):
    return pref if n % pref == 0 else n


def kernel(x_prompt, x_sample, p_prompt, p_sample, state_conv_a, state_conv_b, cache_ckv, cache_krope,
           cache_kscale, page_table, norm_gains, ffn_w_gu, ffn_w_down, ple_w_gate, ple_w_proj, conv_w_in,
           conv_dw_a, conv_dw_a_bias, conv_ln_gain, conv_ln_bias, conv_dw_b, conv_w_out, mla_w_dq, mla_q_norm,
           mla_w_uq, mla_w_dkv, mla_kv_norm, mla_w_uk, mla_w_uv, mla_qk_gain_q, mla_qk_gain_k, mla_w_o):
    bsz, seq, d = x_prompt.shape
    dec = x_sample.shape[0]
    depth = norm_gains.shape[0]
    nh = N_HEADS
    past_len = page_table.shape[1] * PAGE_SIZE
    n_p = bsz * seq

    xp = x_prompt.reshape(n_p, d)
    xs = x_sample.reshape(dec, d)
    bm_p = _tile_rows(n_p, 512)
    bm_ffn = _tile_rows(n_p, 1024)
    bf = _tile_rows(ffn_w_down.shape[2], 512)
    w_gu = ffn_w_gu.astype(BF16)
    w_down = ffn_w_down.astype(BF16)
    w_gate = ple_w_gate.astype(BF16)
    w_proj = ple_w_proj.astype(BF16)
    pp = p_prompt.reshape(depth, n_p, -1)
    ps = p_sample.reshape(depth, dec, -1)

    cos_p, sin_p = _rope_tables(jnp.arange(seq, dtype=jnp.int32))
    cos_s, sin_s = _rope_tables(jnp.full((dec,), past_len, jnp.int32))

    ca_p, ca_s, cb_p, cb_s = [], [], [], []
    ckv_p, ckv_s, kr_p, kr_s, ks_p, ks_s = [], [], [], [], [], []
    for i in range(depth):
        j = i // 2
        gains = norm_gains[i][:, None, :]

        xp = _ffn(xp, gains[0], w_gu, w_down, i, 0, bm_ffn, bf)
        xs = _ffn(xs, gains[0], w_gu, w_down, i, 0, dec, bf)

        if i % 2 == 0:
            w_in = conv_w_in[j].astype(BF16)
            w_out = conv_w_out[j].astype(BF16)
            dwa = jnp.concatenate([conv_dw_a[j], jnp.zeros((1, CONV_WIDTH), F32)], axis=0)
            dwb = jnp.concatenate([conv_dw_b[j], jnp.zeros((8 - KB, CONV_WIDTH), F32)], axis=0)
            ba, lng, lnb = conv_dw_a_bias[j][None, :], conv_ln_gain[j][None, :], conv_ln_bias[j][None, :]

            ua, ub, gb = _conv_in(xp, gains[1], w_in, bm_p, 512)
            ua3, ub3 = ua.reshape(bsz, seq, CONV_WIDTH), ub.reshape(bsz, seq, CONV_WIDTH)
            xp = _conv_prompt(ua3, ub3, gb.reshape(bsz, seq, CONV_WIDTH), xp.reshape(bsz, seq, d),
                              dwa, ba, lng, lnb, dwb, w_out, _tile_rows(seq, 512)).reshape(n_p, d)
            ca_p.append(ua3[:, seq - (KA - 1):])
            cb_p.append(ub3[:, seq - (KB - 1):])

            ua, ub, gb = _conv_in(xs, gains[1], w_in, dec, 512)
            sa, sb = state_conv_a[j], state_conv_b[j]
            xs = _conv_sample(jnp.swapaxes(sa, 0, 1), ua, jnp.swapaxes(sb, 0, 1), ub, gb, xs,
                              dwa, ba, lng, lnb, dwb, w_out)
            ca_s.append(jnp.concatenate([sa[:, 1:], ua[:, None, :]], axis=1))
            cb_s.append(jnp.concatenate([sb[:, 1:], ub[:, None, :]], axis=1))
        else:
            weights = _mla_weights(mla_w_dq[j], mla_q_norm[j], mla_w_uq[j], mla_w_dkv[j], mla_kv_norm[j],
                                   mla_w_uk[j], mla_qk_gain_q[j], mla_qk_gain_k[j])
            w_uv_h = jnp.transpose(mla_w_uv[j], (1, 0, 2)).astype(BF16)
            w_o = mla_w_o[j].astype(BF16)

            bmq = _tile_rows(seq, 256)
            ckv, kr, ks, ckvb, krb, ql, qr = _mla_proj(xp, gains[1], cos_p, sin_p, weights, bmq)
            kst = jnp.swapaxes(ks.reshape(bsz, seq, nh), 1, 2)
            o_lat = _attn_prompt(ql.reshape(nh, bsz, seq, KV_LORA), qr.reshape(nh, bsz, seq, LANES),
                                 ckvb.reshape(bsz, seq, KV_LORA), krb.reshape(bsz, seq, LANES), kst,
                                 _tile_rows(seq, 512))
            xp = _mla_out(o_lat.reshape(nh, n_p, KV_LORA), xp, w_uv_h, w_o, bm_p)
            npg = seq // PAGE_SIZE
            ckv_p.append(ckv.reshape(bsz, npg, PAGE_SIZE, KV_LORA))
            kr_p.append(kr.reshape(bsz, npg, PAGE_SIZE, QK_ROPE))
            ks_p.append(ks.reshape(bsz, npg, PAGE_SIZE, nh))

            ckv, kr, ks, ckvb, krb, ql, qr = _mla_proj(xs, gains[1], cos_s, sin_s, weights, dec)
            o_lat = _attn_sample(page_table, jnp.swapaxes(ql, 0, 1), jnp.swapaxes(qr, 0, 1), ckvb[:, None, :],
                                 krb[:, None, :], ks[:, :, None], cache_ckv[j],
                                 jnp.swapaxes(cache_krope[j], 1, 2), jnp.swapaxes(cache_kscale[j], 1, 2))
            xs = _mla_out(jnp.swapaxes(o_lat, 0, 1), xs, w_uv_h, w_o, dec)
            ckv_s.append(ckv[:, None, :])
            kr_s.append(kr[:, None, :])
            ks_s.append(ks[:, None, :])

        xp = _ffn(xp, gains[2], w_gu, w_down, i, 1, bm_ffn, bf)
        xs = _ffn(xs, gains[2], w_gu, w_down, i, 1, dec, bf)
        xp = _ple(xp, gains[3], w_gate, pp, w_proj, i, bm_p)
        xs = _ple(xs, gains[3], w_gate, ps, w_proj, i, dec)

    return (xp.reshape(bsz, seq, d), xs.reshape(dec, 1, d),
            jnp.stack(ca_p), jnp.stack(ca_s), jnp.stack(cb_p), jnp.stack(cb_s),
            jnp.stack(ckv_p), jnp.stack(ckv_s), jnp.stack(kr_p), jnp.stack(kr_s),
            jnp.stack(ks_p), jnp.stack(ks_s))
```
